```python
import math
import jax, jax.numpy as jnp
from jax import lax
import numpy as np

D_MODEL = 1024
BATCH = 8
SEQ = 4096
DEPTH = 4

HEAD_DIM = 64
N_A_LAYERS = DEPTH // 2
N_B_LAYERS = DEPTH - N_A_LAYERS
DIFF_HEADS = D_MODEL // (2 * HEAD_DIM)
SWA_Q_HEADS = D_MODEL // HEAD_DIM
SWA_KV_HEADS = 2
SWA_GROUP = SWA_Q_HEADS // SWA_KV_HEADS
KV_WIDTH = SWA_KV_HEADS * HEAD_DIM
WINDOW = 128
Q_BLOCK = 128
D_FF = 2816
CONV_WIDTH = 3
NORM_EPS = 1e-6
SUBLN_EPS = 1e-5

kernel_name = "yoco_diffattn_swa_sinks_convffn"


def alibi_slopes(n):
    return np.asarray([2.0 ** (-8.0 * (i + 1) / n) for i in range(n)], dtype=np.float32)


def rmsnorm(x, g, eps=NORM_EPS):
    xf = x.astype(jnp.float32)
    y = xf * lax.rsqrt(jnp.mean(xf * xf, axis=-1, keepdims=True) + eps)
    return (y * g.astype(jnp.float32)).astype(x.dtype)


def diff_attention(h, w_qkv, lam_qk, subln_g, w_o, lambda_init):
    B, S, _ = h.shape
    H, d = DIFF_HEADS, HEAD_DIM
    q, k, v = jnp.split(h @ w_qkv, 3, axis=-1)
    q = q.reshape(B, S, 2 * H, d)
    k = k.reshape(B, S, 2 * H, d)
    v = v.reshape(B, S, H, 2 * d)
    lq = lam_qk.astype(jnp.float32)
    lam = jnp.exp(jnp.sum(lq[0] * lq[1])) - jnp.exp(jnp.sum(lq[2] * lq[3])) + lambda_init
    slopes = jnp.asarray(np.repeat(alibi_slopes(H), 2))
    scale = d ** -0.5
    nblk = S // Q_BLOCK
    qb = q.reshape(B, nblk, Q_BLOCK, 2 * H, d).transpose(1, 0, 2, 3, 4)
    key_pos = jnp.arange(S)

    def block(args):
        q_blk, blk = args
        q_pos = blk * Q_BLOCK + jnp.arange(Q_BLOCK)
        dist = q_pos[:, None] - key_pos[None, :]
        s = jnp.einsum('bqhd,bkhd->bhqk', q_blk, k).astype(jnp.float32) * scale
        s = s - slopes[:, None, None] * dist.astype(jnp.float32)
        s = jnp.where(dist >= 0, s, -jnp.inf)
        p = jax.nn.softmax(s, axis=-1).reshape(B, H, 2, Q_BLOCK, S)
        a = (p[:, :, 0] - lam * p[:, :, 1]).astype(v.dtype)
        return jnp.einsum('bhqk,bkhe->bqhe', a, v)

    o = lax.map(block, (qb, jnp.arange(nblk)))
    o = o.transpose(1, 0, 2, 3, 4).reshape(B, S, H, 2 * d)
    o = rmsnorm(o, subln_g, SUBLN_EPS) * (1.0 - lambda_init)
    return o.reshape(B, S, H * 2 * d) @ w_o


def shared_kv(x, kv_norm, w_kv, b_kv):
    B, S, _ = x.shape
    nblk = S // WINDOW
    kv = rmsnorm(x, kv_norm) @ w_kv + b_kv
    k, v = jnp.split(kv, 2, axis=-1)

    def band(t):
        t = t.reshape(B, nblk, WINDOW, SWA_KV_HEADS, HEAD_DIM)
        prev = jnp.concatenate([jnp.zeros_like(t[:, :1]), t[:, :-1]], axis=1)
        return jnp.concatenate([prev, t], axis=2)

    return band(k), band(v)


def swa_sink_attention(h, k_band, v_band, w_q, b_q, sinks, w_o):
    B, S, _ = h.shape
    nblk = S // WINDOW
    q = (h @ w_q + b_q).reshape(B, nblk, WINDOW, SWA_KV_HEADS, SWA_GROUP, HEAD_DIM)
    s = jnp.einsum('bnqkgd,bnjkd->bnkgqj', q, k_band).astype(jnp.float32) * (HEAD_DIM ** -0.5)
    qi = jnp.arange(WINDOW)
    kj = jnp.arange(2 * WINDOW)
    dist = WINDOW + qi[:, None] - kj[None, :]
    valid = (dist >= 0) & (dist < WINDOW)
    mask = valid[None] & ((jnp.arange(nblk)[:, None, None] > 0) | (kj[None, None, :] >= WINDOW))
    slopes = jnp.asarray(alibi_slopes(SWA_Q_HEADS).reshape(SWA_KV_HEADS, SWA_GROUP))
    s = s - slopes[:, :, None, None] * dist.astype(jnp.float32)
    s = jnp.where(mask[None, :, None, None], s, -jnp.inf)
    sink = jnp.broadcast_to(
        sinks.astype(jnp.float32).reshape(SWA_KV_HEADS, SWA_GROUP)[:, :, None, None],
        s.shape[:-1] + (1,))
    p = jax.nn.softmax(jnp.concatenate([s, sink], axis=-1), axis=-1)[..., :-1]
    o = jnp.einsum('bnkgqj,bnjkd->bnqkgd', p.astype(v_band.dtype), v_band)
    return o.reshape(B, S, SWA_Q_HEADS * HEAD_DIM) @ w_o


def conv_ffn(h, w_up, conv_w, conv_b, w_down):
    gate, up = jnp.split(h @ w_up, 2, axis=-1)
    c = conv_w[CONV_WIDTH - 1] * gate + conv_b
    for sh in range(1, CONV_WIDTH):
        shifted = jnp.pad(gate[:, :-sh], ((0, 0), (sh, 0), (0, 0)))
        c = c + conv_w[CONV_WIDTH - 1 - sh] * shifted
    return (jax.nn.gelu(c, approximate=True) * up) @ w_down


def setup_inputs(seed: int = 0) -> dict:
    key = jax.random.key(seed)
    ks = jax.random.split(key, 20)
    D, d = D_MODEL, HEAD_DIM
    nrm = lambda k, shape, scale: jax.random.normal(k, shape, jnp.float32) * scale
    return {
        "x": nrm(ks[0], (BATCH, SEQ, D), 1.0),
        "norm_gains": 1.0 + nrm(ks[1], (DEPTH, 4, D), 0.1),
        "w_qkv_a": nrm(ks[2], (N_A_LAYERS, D, 3 * D), D ** -0.5),
        "lambda_qk_a": nrm(ks[3], (N_A_LAYERS, 4, d), 0.1),
        "subln_a": 1.0 + nrm(ks[4], (N_A_LAYERS, 2 * d), 0.1),
        "w_o_a": nrm(ks[5], (N_A_LAYERS, D, D), D ** -0.5),
        "kv_norm": 1.0 + nrm(ks[6], (D,), 0.1),
        "w_kv_b": nrm(ks[7], (D, 2 * KV_WIDTH), D ** -0.5),
        "b_kv_b": nrm(ks[8], (2 * KV_WIDTH,), 0.01),
        "w_q_b": nrm(ks[9], (N_B_LAYERS, D, SWA_Q_HEADS * d), D ** -0.5),
        "b_q_b": nrm(ks[10], (N_B_LAYERS, SWA_Q_HEADS * d), 0.01),
        "sinks_b": nrm(ks[11], (N_B_LAYERS, SWA_Q_HEADS), 0.5),
        "w_o_b": nrm(ks[12], (N_B_LAYERS, SWA_Q_HEADS * d, D), (SWA_Q_HEADS * d) ** -0.5),
        "w_up": nrm(ks[13], (DEPTH, D, 2 * D_FF), D ** -0.5),
        "conv_w": nrm(ks[14], (DEPTH, CONV_WIDTH, D_FF), CONV_WIDTH ** -0.5),
        "conv_b": nrm(ks[15], (DEPTH, D_FF), 0.01),
        "w_down": nrm(ks[16], (DEPTH, D_FF, D), D_FF ** -0.5),
    }


def reference(x, norm_gains, w_qkv_a, lambda_qk_a, subln_a, w_o_a, kv_norm, w_kv_b, b_kv_b,
              w_q_b, b_q_b, sinks_b, w_o_b, w_up, conv_w, conv_b, w_down):
    k_band = v_band = None
    for layer in range(DEPTH):
        g = norm_gains[layer]
        h = rmsnorm(x, g[0])
        if layer < N_A_LAYERS:
            lambda_init = 0.8 - 0.6 * math.exp(-0.3 * layer)
            mix = diff_attention(h, w_qkv_a[layer], lambda_qk_a[layer], subln_a[layer],
                                 w_o_a[layer], lambda_init)
        else:
            if layer == N_A_LAYERS:
                k_band, v_band = shared_kv(x, kv_norm, w_kv_b, b_kv_b)
            j = layer - N_A_LAYERS
            mix = swa_sink_attention(h, k_band, v_band, w_q_b[j], b_q_b[j], sinks_b[j], w_o_b[j])
        x = x + rmsnorm(mix, g[1])
        f = conv_ffn(rmsnorm(x, g[2]), w_up[layer], conv_w[layer], conv_b[layer], w_down[layer])
        x = x + rmsnorm(f, g[3])
    return x
```

```python
import functools
import math

import numpy as np
import jax
import jax.numpy as jnp
from jax import lax
from jax.experimental import pallas as pl
from jax.experimental.pallas import tpu as pltpu

D_MODEL = 1024
HEAD_DIM = 64
DEPTH = 4
N_A_LAYERS = DEPTH // 2
DIFF_HEADS = D_MODEL // (2 * HEAD_DIM)
SWA_Q_HEADS = D_MODEL // HEAD_DIM
SWA_KV_HEADS = 2
SWA_GROUP = SWA_Q_HEADS // SWA_KV_HEADS
WINDOW = 128
D_FF = 2816
CONV_WIDTH = 3
NORM_EPS = 1e-6
SUBLN_EPS = 1e-5

LANES = 128
QK_SCALE = HEAD_DIM ** -0.5
VMEM_LIMIT_BYTES = 56 * 1024 * 1024
ROW_TILE = 512
ATTN_TILE = 512
FF_CHUNK = 256
POS_SPLIT = 64

BF16 = jnp.bfloat16
F32 = jnp.float32


def _alibi_slopes(n):
    return [2.0 ** (-8.0 * (i + 1) / n) for i in range(n)]


def _rms(x, g, eps):
    return x * lax.rsqrt(jnp.mean(x * x, axis=-1, keepdims=True) + eps) * g


def _params(n_axes):
    return pltpu.CompilerParams(
        dimension_semantics=("arbitrary",) * n_axes, vmem_limit_bytes=VMEM_LIMIT_BYTES)


def _resident(shape):
    nd = len(shape)
    return pl.BlockSpec(shape, lambda *_: (0,) * nd, pipeline_mode=pl.Buffered(1))


def _proj_kernel(x_ref, g_ref, w_ref, b_ref, o_ref, *, n_slabs, dot_slabs):
    h = _rms(x_ref[...], g_ref[...], NORM_EPS).astype(BF16)
    for c0 in range(0, n_slabs, dot_slabs):
        nc = min(dot_slabs, n_slabs - c0)
        cols = slice(c0 * LANES, (c0 + nc) * LANES)
        y = jnp.dot(h, w_ref[:, cols], preferred_element_type=F32) + b_ref[:, cols]
        for c in range(nc):
            o_ref[c0 + c] = y[:, c * LANES:(c + 1) * LANES].astype(BF16)


def _proj(x, g, w, b):
    T, D = x.shape
    N = w.shape[1]
    n_slabs = N // LANES
    tm = ROW_TILE
    return pl.pallas_call(
        functools.partial(_proj_kernel, n_slabs=n_slabs, dot_slabs=4),
        grid=(T // tm,),
        in_specs=[
            pl.BlockSpec((tm, D), lambda i: (i, 0)),
            _resident((1, D)),
            _resident((D, N)),
            _resident((1, N)),
        ],
        out_specs=pl.BlockSpec((n_slabs, tm, LANES), lambda i: (0, i, 0)),
        out_shape=jax.ShapeDtypeStruct((n_slabs, T, LANES), BF16),
        compiler_params=_params(1),
        name="norm_proj",
    )(x, g.reshape(1, D), w, b.reshape(1, N))


def _attn_a_kernel(q_ref, k_ref, v_ref, kpos_ref, lq_ref, sg_ref, o_ref,
                   kk_scr, m_scr, l_scr, acc_scr, *, tq, lambda_init):
    qi = pl.program_id(2)

    @pl.when(qi == 0)
    def _():
        kk_scr[:, :LANES] = k_ref[...]
        kk_scr[:, LANES:] = kpos_ref[...]

    q = q_ref[...] * QK_SCALE
    lane = lax.broadcasted_iota(jnp.int32, (tq, LANES), 1)
    zero = jnp.zeros_like(q)
    ones_cols = jnp.where(lane < 2, 1.0, 0.0).astype(BF16)
    qq = jnp.concatenate([
        jnp.concatenate([jnp.where(lane < HEAD_DIM, q, zero), ones_cols], axis=1),
        jnp.concatenate([jnp.where(lane >= HEAD_DIM, q, zero), ones_cols], axis=1)], axis=0)

    m_scr[...] = jnp.full(m_scr.shape, -jnp.inf, F32)
    l_scr[...] = jnp.zeros(l_scr.shape, F32)
    acc_scr[...] = jnp.zeros(acc_scr.shape, F32)

    def tile(ki, diagonal):
        start = pl.multiple_of(ki * tq, tq)
        kk = kk_scr[pl.ds(start, tq), :]
        s = lax.dot_general(qq, kk, (((1,), (1,)), ((), ())), preferred_element_type=F32)
        if diagonal:
            r = lax.broadcasted_iota(jnp.int32, (2 * tq, tq), 0)
            c = lax.broadcasted_iota(jnp.int32, (2 * tq, tq), 1)
            s = jnp.where(c <= (r & (tq - 1)), s, -jnp.inf)
        m_prev = m_scr[...]
        m_new = jnp.maximum(m_prev, jnp.max(s, axis=-1, keepdims=True))
        alpha = jnp.exp(m_prev - m_new)
        p = jnp.exp(s - m_new)
        l_scr[...] = alpha * l_scr[...] + jnp.sum(p, axis=-1, keepdims=True)
        acc_scr[...] = alpha * acc_scr[...] + jnp.dot(
            p.astype(BF16), v_ref[pl.ds(start, tq), :], preferred_element_type=F32)
        m_scr[...] = m_new

    def body(ki, carry):
        tile(ki, False)
        return carry

    lax.fori_loop(0, qi, body, 0)
    tile(qi, True)

    lq = lq_ref[...]
    lam = (jnp.exp(jnp.sum(lq[0:1] * lq[1:2], axis=-1, keepdims=True))
           - jnp.exp(jnp.sum(lq[2:3] * lq[3:4], axis=-1, keepdims=True)) + lambda_init)
    o = acc_scr[:tq] / l_scr[:tq] - lam * (acc_scr[tq:] / l_scr[tq:])
    o_ref[...] = (_rms(o, sg_ref[...], SUBLN_EPS) * (1.0 - lambda_init)).astype(BF16)


def _alibi_key_columns(S):
    assert S <= POS_SPLIT * 256
    j = np.arange(S)
    cols = np.zeros((DIFF_HEADS, S, LANES), np.float32)
    for h, slope in enumerate(_alibi_slopes(DIFF_HEADS)):
        cols[h, :, 0] = slope * POS_SPLIT * (j // POS_SPLIT)
        cols[h, :, 1] = slope * (j % POS_SPLIT)
    return jnp.asarray(cols, BF16)


def _attn_a(qkv3, lam_qk, subln_g, lambda_init, B, S):
    T = B * S
    H = DIFF_HEADS
    tq = ATTN_TILE
    nq = S // tq
    kernel = functools.partial(_attn_a_kernel, tq=tq, lambda_init=lambda_init)
    return pl.pallas_call(
        kernel,
        grid=(B, H, nq),
        in_specs=[
            pl.BlockSpec((None, tq, LANES), lambda b, h, qi: (h, b * nq + qi, 0)),
            pl.BlockSpec((None, S, LANES), lambda b, h, qi: (H + h, b, 0)),
            pl.BlockSpec((None, S, LANES), lambda b, h, qi: (2 * H + h, b, 0)),
            pl.BlockSpec((None, S, LANES), lambda b, h, qi: (h, 0, 0)),
            _resident((4, HEAD_DIM)),
            _resident((1, LANES)),
        ],
        out_specs=pl.BlockSpec((tq, LANES), lambda b, h, qi: (b * nq + qi, h)),
        out_shape=jax.ShapeDtypeStruct((T, D_MODEL), BF16),
        scratch_shapes=[
            pltpu.VMEM((S, 2 * LANES), BF16),
            pltpu.VMEM((2 * tq, 1), F32),
            pltpu.VMEM((2 * tq, 1), F32),
            pltpu.VMEM((2 * tq, LANES), F32),
        ],
        compiler_params=_params(3),
        name="diff_attn",
    )(qkv3, qkv3, qkv3, _alibi_key_columns(S), lam_qk, subln_g.reshape(1, LANES))


def _attn_b_kernel(sinks_ref, q_ref, k_ref, v_ref, o_ref):
    n = pl.program_id(1)
    start = pl.multiple_of(jnp.maximum(n - 1, 0) * WINDOW, WINDOW)
    offset = n * WINDOW - start
    r = lax.broadcasted_iota(jnp.int32, (WINDOW, 2 * WINDOW), 0)
    c = lax.broadcasted_iota(jnp.int32, (WINDOW, 2 * WINDOW), 1)
    dist = offset + r - c
    valid = (dist >= 0) & (dist < WINDOW)
    distf = dist.astype(F32)
    lane = lax.broadcasted_iota(jnp.int32, (WINDOW, LANES), 1)
    slopes = _alibi_slopes(SWA_Q_HEADS)
    pairs_per_kv = SWA_GROUP // 2

    for kh in range(SWA_KV_HEADS):
        kwin = k_ref[kh, pl.ds(start, 2 * WINDOW), :]
        vwin = v_ref[kh, pl.ds(start, 2 * WINDOW), :]
        rows = []
        for cpair in range(pairs_per_kv):
            qc = q_ref[kh * pairs_per_kv + cpair] * QK_SCALE
            zero = jnp.zeros_like(qc)
            rows.append(jnp.where(lane < HEAD_DIM, qc, zero))
            rows.append(jnp.where(lane >= HEAD_DIM, qc, zero))
        s = lax.dot_general(jnp.concatenate(rows, axis=0), kwin, (((1,), (1,)), ((), ())),
                            preferred_element_type=F32)
        probs, denoms = [], []
        for g in range(SWA_GROUP):
            head = kh * SWA_GROUP + g
            sg = s[g * WINDOW:(g + 1) * WINDOW] - slopes[head] * distf
            sg = jnp.where(valid, sg, -jnp.inf)
            sink = sinks_ref[head]
            m = jnp.maximum(jnp.max(sg, axis=-1, keepdims=True), sink)
            e = jnp.exp(sg - m)
            denoms.append(jnp.sum(e, axis=-1, keepdims=True) + jnp.exp(sink - m))
            probs.append(e.astype(BF16))
        o = jnp.dot(jnp.concatenate(probs, axis=0), vwin, preferred_element_type=F32)
        for cpair in range(pairs_per_kv):
            ge, go = 2 * cpair, 2 * cpair + 1
            oe = o[ge * WINDOW:(ge + 1) * WINDOW] / denoms[ge]
            oo = o[go * WINDOW:(go + 1) * WINDOW] / denoms[go]
            col = (kh * pairs_per_kv + cpair) * LANES
            o_ref[:, col:col + LANES] = jnp.where(lane < HEAD_DIM, oe, oo).astype(BF16)


def _attn_b(q3, kv3, sinks, B, S):
    T = B * S
    nblk = S // WINDOW
    n_pairs = SWA_Q_HEADS // 2
    return pl.pallas_call(
        _attn_b_kernel,
        grid_spec=pltpu.PrefetchScalarGridSpec(
            num_scalar_prefetch=1,
            grid=(B, nblk),
            in_specs=[
                pl.BlockSpec((n_pairs, WINDOW, LANES), lambda b, n, *_: (0, b * nblk + n, 0)),
                pl.BlockSpec((SWA_KV_HEADS, S, LANES), lambda b, n, *_: (0, b, 0)),
                pl.BlockSpec((SWA_KV_HEADS, S, LANES), lambda b, n, *_: (1, b, 0)),
            ],
            out_specs=pl.BlockSpec((WINDOW, D_MODEL), lambda b, n, *_: (b * nblk + n, 0)),
        ),
        out_shape=jax.ShapeDtypeStruct((T, D_MODEL), BF16),
        compiler_params=_params(2),
        name="swa_attn",
    )(sinks, q3, kv3, kv3)


def _post_kernel(o_ref, w_ref, x_ref, g_ref, out_ref):
    mix = jnp.dot(o_ref[...], w_ref[...], preferred_element_type=F32)
    out_ref[...] = x_ref[...] + _rms(mix, g_ref[...], NORM_EPS)


def _post(o, w, x, g):
    T, D = x.shape
    tm = ROW_TILE
    return pl.pallas_call(
        _post_kernel,
        grid=(T // tm,),
        in_specs=[
            pl.BlockSpec((tm, D), lambda i: (i, 0)),
            _resident((D, D)),
            pl.BlockSpec((tm, D), lambda i: (i, 0)),
            _resident((1, D)),
        ],
        out_specs=pl.BlockSpec((tm, D), lambda i: (i, 0)),
        out_shape=jax.ShapeDtypeStruct((T, D), F32),
        input_output_aliases={2: 0},
        compiler_params=_params(1),
        name="out_proj_residual",
    )(o, w, x, g.reshape(1, D))


def _gelu_tanh(x):
    return 0.5 * x * (1.0 + jnp.tanh(math.sqrt(2.0 / math.pi) * (x + 0.044715 * (x * x * x))))


def _ffn_kernel(x_ref, g2_ref, g3_ref, wgu_ref, wd_ref, cw_ref, cb_ref, out_ref,
                h_scr, acc_scr, carry_scr, *, tiles_per_seq, n_chunks):
    i = pl.program_id(0)
    tm = x_ref.shape[0]
    tf = FF_CHUNK

    @pl.when(i == 0)
    def _():
        carry_scr[...] = jnp.zeros(carry_scr.shape, F32)

    x = x_ref[...]
    h_scr[...] = _rms(x, g2_ref[...], NORM_EPS).astype(BF16)
    acc_scr[...] = jnp.zeros(acc_scr.shape, F32)
    seq_start = (i % tiles_per_seq) == 0
    row = lax.broadcasted_iota(jnp.int32, (tm, tf), 0)

    def chunk(j, carry):
        gu = jnp.dot(h_scr[...], wgu_ref[j], preferred_element_type=F32)
        gate, up = gu[:, :tf], gu[:, tf:]
        prev = jnp.where(seq_start, 0.0, carry_scr[j])
        carry_scr[j] = gate[tm - 8:, :]
        p1 = jnp.broadcast_to(prev[7:8], (tm, tf))
        p2 = jnp.broadcast_to(prev[6:7], (tm, tf))
        g1 = jnp.where(row >= 1, pltpu.roll(gate, 1, 0), p1)
        g2 = jnp.where(row >= 2, pltpu.roll(gate, 2, 0), jnp.where(row == 1, p1, p2))
        cw = cw_ref[j]
        conv = cw[2:3] * gate + cb_ref[j] + cw[1:2] * g1 + cw[0:1] * g2
        a = (_gelu_tanh(conv) * up).astype(BF16)
        acc_scr[...] += jnp.dot(a, wd_ref[j], preferred_element_type=F32)
        return carry

    lax.fori_loop(0, n_chunks, chunk, 0)
    out_ref[...] = x + _rms(acc_scr[...], g3_ref[...], NORM_EPS)


def _ffn(x, g2, g3, w_up, conv_w, conv_b, w_down, S):
    T, D = x.shape
    tm = ROW_TILE
    tf = FF_CHUNK
    nc = D_FF // tf
    wg = w_up[:, :D_FF].reshape(D, nc, tf)
    wu = w_up[:, D_FF:].reshape(D, nc, tf)
    wgu = jnp.concatenate([wg, wu], axis=2).transpose(1, 0, 2).astype(BF16)
    wd = w_down.reshape(nc, tf, D).astype(BF16)
    cw = conv_w.reshape(CONV_WIDTH, nc, tf).transpose(1, 0, 2)
    cb = conv_b.reshape(nc, 1, tf)
    kernel = functools.partial(_ffn_kernel, tiles_per_seq=S // tm, n_chunks=nc)
    return pl.pallas_call(
        kernel,
        grid=(T // tm,),
        in_specs=[
            pl.BlockSpec((tm, D), lambda i: (i, 0)),
            _resident((1, D)),
            _resident((1, D)),
            _resident((nc, D, 2 * tf)),
            _resident((nc, tf, D)),
            _resident((nc, CONV_WIDTH, tf)),
            _resident((nc, 1, tf)),
        ],
        out_specs=pl.BlockSpec((tm, D), lambda i: (i, 0)),
        out_shape=jax.ShapeDtypeStruct((T, D), F32),
        scratch_shapes=[
            pltpu.VMEM((tm, D), BF16),
            pltpu.VMEM((tm, D), F32),
            pltpu.VMEM((nc, 8, tf), F32),
        ],
        input_output_aliases={0: 0},
        compiler_params=_params(1),
        name="conv_ffn",
    )(x, g2.reshape(1, D), g3.reshape(1, D), wgu, wd, cw, cb)


def _dup_heads(w, n_heads):
    lead = w.shape[:-1]
    w = w.reshape(lead + (n_heads, 1, HEAD_DIM))
    return jnp.broadcast_to(w, lead + (n_heads, 2, HEAD_DIM)).reshape(lead + (2 * n_heads * HEAD_DIM,))


def kernel(x, norm_gains, w_qkv_a, lambda_qk_a, subln_a, w_o_a, kv_norm, w_kv_b, b_kv_b, w_q_b, b_q_b,
           sinks_b, w_o_b, w_up, conv_w, conv_b, w_down):
    B, S, D = x.shape
    T = B * S
    assert D == D_MODEL and S % ATTN_TILE == 0 and S % ROW_TILE == 0 and S >= 2 * WINDOW
    xt = x.reshape(T, D)
    kv3 = None
    for layer in range(DEPTH):
        g = norm_gains[layer]
        if layer < N_A_LAYERS:
            lambda_init = 0.8 - 0.6 * math.exp(-0.3 * layer)
            qkv3 = _proj(xt, g[0], w_qkv_a[layer].astype(BF16), jnp.zeros((3 * D,), F32))
            o = _attn_a(qkv3, lambda_qk_a[layer], subln_a[layer], lambda_init, B, S)
            xt = _post(o, w_o_a[layer].astype(BF16), xt, g[1])
        else:
            j = layer - N_A_LAYERS
            if kv3 is None:
                kvw = SWA_KV_HEADS * HEAD_DIM
                w_kv = jnp.concatenate([_dup_heads(w_kv_b[:, :kvw], SWA_KV_HEADS),
                                        _dup_heads(w_kv_b[:, kvw:], SWA_KV_HEADS)], axis=1)
                b_kv = jnp.concatenate([_dup_heads(b_kv_b[:kvw], SWA_KV_HEADS),
                                        _dup_heads(b_kv_b[kvw:], SWA_KV_HEADS)])
                kv3 = _proj(xt, kv_norm, w_kv.astype(BF16), b_kv)
            q3 = _proj(xt, g[0], w_q_b[j].astype(BF16), b_q_b[j])
            o = _attn_b(q3, kv3, sinks_b[j], B, S)
            xt = _post(o, w_o_b[j].astype(BF16), xt, g[1])
        xt = _ffn(xt, g[2], g[3], w_up[layer], conv_w[layer], conv_b[layer], w_down[layer], S)
    return xt.reshape(B, S, D)
```

```python
import functools
import math

import numpy as np
import jax
import jax.numpy as jnp
from jax import lax
from jax.experimental import pallas as pl
from jax.experimental.pallas import tpu as pltpu

D_MODEL = 1024
HEAD_DIM = 64
DEPTH = 4
N_A_LAYERS = DEPTH // 2
DIFF_HEADS = D_MODEL // (2 * HEAD_DIM)
SWA_Q_HEADS = D_MODEL // HEAD_DIM
SWA_KV_HEADS = 2
SWA_GROUP = SWA_Q_HEADS // SWA_KV_HEADS
WINDOW = 128
D_FF = 2816
CONV_WIDTH = 3
NORM_EPS = 1e-6
SUBLN_EPS = 1e-5

LANES = 128
QK_SCALE = HEAD_DIM ** -0.5
VMEM_LIMIT_BYTES = 56 * 1024 * 1024
ROW_TILE = 512
ATTN_TILE = 512
FF_CHUNK = 256
POS_SPLIT = 64

BF16 = jnp.bfloat16
F32 = jnp.float32


def _alibi_slopes(n):
    return [2.0 ** (-8.0 * (i + 1) / n) for i in range(n)]


def _rms(x, g, eps):
    return x * lax.rsqrt(jnp.mean(x * x, axis=-1, keepdims=True) + eps) * g


def _params(n_axes):
    return pltpu.CompilerParams(
        dimension_semantics=("arbitrary",) * n_axes, vmem_limit_bytes=VMEM_LIMIT_BYTES)


def _resident(shape):
    nd = len(shape)
    return pl.BlockSpec(shape, lambda *_: (0,) * nd, pipeline_mode=pl.Buffered(1))


def _proj_kernel(x_ref, g_ref, w_ref, b_ref, o_ref, *, n_slabs, dot_slabs):
    h = _rms(x_ref[...], g_ref[...], NORM_EPS).astype(BF16)
    for c0 in range(0, n_slabs, dot_slabs):
        nc = min(dot_slabs, n_slabs - c0)
        cols = slice(c0 * LANES, (c0 + nc) * LANES)
        y = jnp.dot(h, w_ref[:, cols], preferred_element_type=F32) + b_ref[:, cols]
        for c in range(nc):
            o_ref[c0 + c] = y[:, c * LANES:(c + 1) * LANES].astype(BF16)


def _proj(x, g, w, b):
    T, D = x.shape
    N = w.shape[1]
    n_slabs = N // LANES
    tm = ROW_TILE
    return pl.pallas_call(
        functools.partial(_proj_kernel, n_slabs=n_slabs, dot_slabs=4),
        grid=(T // tm,),
        in_specs=[
            pl.BlockSpec((tm, D), lambda i: (i, 0)),
            _resident((1, D)),
            _resident((D, N)),
            _resident((1, N)),
        ],
        out_specs=pl.BlockSpec((n_slabs, tm, LANES), lambda i: (0, i, 0)),
        out_shape=jax.ShapeDtypeStruct((n_slabs, T, LANES), BF16),
        compiler_params=_params(1),
        name="norm_proj",
    )(x, g.reshape(1, D), w, b.reshape(1, N))


def _attn_a_kernel(q_ref, k_ref, v_ref, kpos_ref, lq_ref, sg_ref, o_ref,
                   kk_scr, vt_scr, m_scr, l_scr, acc_scr, *, tq, lambda_init):
    qi = pl.program_id(2)
    n_tiles = vt_scr.shape[0]

    @pl.when(qi == 0)
    def _():
        kk_scr[:, :LANES] = k_ref[...]
        kk_scr[:, LANES:] = kpos_ref[...]
        for t in range(n_tiles):
            vt_scr[t] = v_ref[t * tq:(t + 1) * tq, :].T

    q = q_ref[...] * QK_SCALE
    lane = lax.broadcasted_iota(jnp.int32, (tq, LANES), 1)
    zero = jnp.zeros_like(q)
    ones_cols = jnp.where(lane < 2, 1.0, 0.0).astype(BF16)
    qq = jnp.concatenate([
        jnp.concatenate([jnp.where(lane < HEAD_DIM, q, zero), ones_cols], axis=1),
        jnp.concatenate([jnp.where(lane >= HEAD_DIM, q, zero), ones_cols], axis=1)], axis=0)

    m_scr[...] = jnp.full(m_scr.shape, -jnp.inf, F32)
    l_scr[...] = jnp.zeros(l_scr.shape, F32)
    acc_scr[...] = jnp.zeros(acc_scr.shape, F32)

    def tile(ki, diagonal):
        start = pl.multiple_of(ki * tq, tq)
        kk = kk_scr[pl.ds(start, tq), :]
        s = lax.dot_general(kk, qq, (((1,), (1,)), ((), ())), preferred_element_type=F32)
        if diagonal:
            key = lax.broadcasted_iota(jnp.int32, (tq, 2 * tq), 0)
            qry = lax.broadcasted_iota(jnp.int32, (tq, 2 * tq), 1) & (tq - 1)
            s = jnp.where(key <= qry, s, -jnp.inf)
        m_prev = m_scr[...]
        m_new = jnp.maximum(m_prev, jnp.max(s, axis=0, keepdims=True))
        alpha = jnp.exp(m_prev - m_new)
        p = jnp.exp(s - m_new)
        l_scr[...] = alpha * l_scr[...] + jnp.sum(p, axis=0, keepdims=True)
        acc_scr[...] = alpha * acc_scr[...] + jnp.dot(
            vt_scr[ki], p.astype(BF16), preferred_element_type=F32)
        m_scr[...] = m_new

    def body(ki, carry):
        tile(ki, False)
        return carry

    lax.fori_loop(0, qi, body, 0)
    tile(qi, True)

    lq = lq_ref[...]
    lam = (jnp.exp(jnp.sum(lq[0:1] * lq[1:2], axis=-1, keepdims=True))
           - jnp.exp(jnp.sum(lq[2:3] * lq[3:4], axis=-1, keepdims=True)) + lambda_init)
    o = acc_scr[:, :tq] / l_scr[:, :tq] - lam * (acc_scr[:, tq:] / l_scr[:, tq:])
    o = o * lax.rsqrt(jnp.mean(o * o, axis=0, keepdims=True) + SUBLN_EPS) * sg_ref[...]
    o_ref[...] = (o * (1.0 - lambda_init)).T.astype(BF16)


def _alibi_key_columns(S):
    assert S <= POS_SPLIT * 256
    j = np.arange(S)
    cols = np.zeros((DIFF_HEADS, S, LANES), np.float32)
    for h, slope in enumerate(_alibi_slopes(DIFF_HEADS)):
        cols[h, :, 0] = slope * POS_SPLIT * (j // POS_SPLIT)
        cols[h, :, 1] = slope * (j % POS_SPLIT)
    return jnp.asarray(cols, BF16)


def _attn_a(qkv3, lam_qk, subln_g, lambda_init, B, S):
    T = B * S
    H = DIFF_HEADS
    tq = ATTN_TILE
    nq = S // tq
    kernel = functools.partial(_attn_a_kernel, tq=tq, lambda_init=lambda_init)
    return pl.pallas_call(
        kernel,
        grid=(B, H, nq),
        in_specs=[
            pl.BlockSpec((None, tq, LANES), lambda b, h, qi: (h, b * nq + qi, 0)),
            pl.BlockSpec((None, S, LANES), lambda b, h, qi: (H + h, b, 0)),
            pl.BlockSpec((None, S, LANES), lambda b, h, qi: (2 * H + h, b, 0)),
            pl.BlockSpec((None, S, LANES), lambda b, h, qi: (h, 0, 0)),
            _resident((4, HEAD_DIM)),
            _resident((LANES, 1)),
        ],
        out_specs=pl.BlockSpec((tq, LANES), lambda b, h, qi: (b * nq + qi, h)),
        out_shape=jax.ShapeDtypeStruct((T, D_MODEL), BF16),
        scratch_shapes=[
            pltpu.VMEM((S, 2 * LANES), BF16),
            pltpu.VMEM((nq, LANES, tq), BF16),
            pltpu.VMEM((1, 2 * tq), F32),
            pltpu.VMEM((1, 2 * tq), F32),
            pltpu.VMEM((LANES, 2 * tq), F32),
        ],
        compiler_params=_params(3),
        name="diff_attn",
    )(qkv3, qkv3, qkv3, _alibi_key_columns(S), lam_qk, subln_g.reshape(LANES, 1))


def _attn_b_kernel(sinks_ref, q_ref, k_ref, v_ref, o_ref):
    n = pl.program_id(1)
    start = pl.multiple_of(jnp.maximum(n - 1, 0) * WINDOW, WINDOW)
    offset = n * WINDOW - start
    r = lax.broadcasted_iota(jnp.int32, (WINDOW, 2 * WINDOW), 0)
    c = lax.broadcasted_iota(jnp.int32, (WINDOW, 2 * WINDOW), 1)
    dist = offset + r - c
    valid = (dist >= 0) & (dist < WINDOW)
    distf = dist.astype(F32)
    lane = lax.broadcasted_iota(jnp.int32, (WINDOW, LANES), 1)
    slopes = _alibi_slopes(SWA_Q_HEADS)
    pairs_per_kv = SWA_GROUP // 2

    for kh in range(SWA_KV_HEADS):
        kwin = k_ref[kh, pl.ds(start, 2 * WINDOW), :]
        vwin = v_ref[kh, pl.ds(start, 2 * WINDOW), :]
        rows = []
        for cpair in range(pairs_per_kv):
            qc = q_ref[kh * pairs_per_kv + cpair] * QK_SCALE
            zero = jnp.zeros_like(qc)
            rows.append(jnp.where(lane < HEAD_DIM, qc, zero))
            rows.append(jnp.where(lane >= HEAD_DIM, qc, zero))
        s = lax.dot_general(jnp.concatenate(rows, axis=0), kwin, (((1,), (1,)), ((), ())),
                            preferred_element_type=F32)
        probs, denoms = [], []
        for g in range(SWA_GROUP):
            head = kh * SWA_GROUP + g
            sg = s[g * WINDOW:(g + 1) * WINDOW] - slopes[head] * distf
            sg = jnp.where(valid, sg, -jnp.inf)
            sink = sinks_ref[head]
            m = jnp.maximum(jnp.max(sg, axis=-1, keepdims=True), sink)
            e = jnp.exp(sg - m)
            denoms.append(jnp.sum(e, axis=-1, keepdims=True) + jnp.exp(sink - m))
            probs.append(e.astype(BF16))
        o = jnp.dot(jnp.concatenate(probs, axis=0), vwin, preferred_element_type=F32)
        for cpair in range(pairs_per_kv):
            ge, go = 2 * cpair, 2 * cpair + 1
            oe = o[ge * WINDOW:(ge + 1) * WINDOW] / denoms[ge]
            oo = o[go * WINDOW:(go + 1) * WINDOW] / denoms[go]
            col = (kh * pairs_per_kv + cpair) * LANES
            o_ref[:, col:col + LANES] = jnp.where(lane < HEAD_DIM, oe, oo).astype(BF16)


def _attn_b(q3, kv3, sinks, B, S):
    T = B * S
    nblk = S // WINDOW
    n_pairs = SWA_Q_HEADS // 2
    return pl.pallas_call(
        _attn_b_kernel,
        grid_spec=pltpu.PrefetchScalarGridSpec(
            num_scalar_prefetch=1,
            grid=(B, nblk),
            in_specs=[
                pl.BlockSpec((n_pairs, WINDOW, LANES), lambda b, n, *_: (0, b * nblk + n, 0)),
                pl.BlockSpec((SWA_KV_HEADS, S, LANES), lambda b, n, *_: (0, b, 0)),
                pl.BlockSpec((SWA_KV_HEADS, S, LANES), lambda b, n, *_: (1, b, 0)),
            ],
            out_specs=pl.BlockSpec((WINDOW, D_MODEL), lambda b, n, *_: (b * nblk + n, 0)),
        ),
        out_shape=jax.ShapeDtypeStruct((T, D_MODEL), BF16),
        compiler_params=_params(2),
        name="swa_attn",
    )(sinks, q3, kv3, kv3)


def _post_kernel(o_ref, w_ref, x_ref, g_ref, out_ref):
    mix = jnp.dot(o_ref[...], w_ref[...], preferred_element_type=F32)
    out_ref[...] = x_ref[...] + _rms(mix, g_ref[...], NORM_EPS)


def _post(o, w, x, g):
    T, D = x.shape
    tm = ROW_TILE
    return pl.pallas_call(
        _post_kernel,
        grid=(T // tm,),
        in_specs=[
            pl.BlockSpec((tm, D), lambda i: (i, 0)),
            _resident((D, D)),
            pl.BlockSpec((tm, D), lambda i: (i, 0)),
            _resident((1, D)),
        ],
        out_specs=pl.BlockSpec((tm, D), lambda i: (i, 0)),
        out_shape=jax.ShapeDtypeStruct((T, D), F32),
        input_output_aliases={2: 0},
        compiler_params=_params(1),
        name="out_proj_residual",
    )(o, w, x, g.reshape(1, D))


def _gelu_tanh(x):
    return 0.5 * x * (1.0 + jnp.tanh(math.sqrt(2.0 / math.pi) * (x + 0.044715 * (x * x * x))))


def _ffn_kernel(x_ref, g2_ref, g3_ref, wgu_ref, wd_ref, cw_ref, cb_ref, out_ref,
                h_scr, acc_scr, carry_scr, *, tiles_per_seq, n_chunks):
    i = pl.program_id(0)
    tm = x_ref.shape[0]
    tf = FF_CHUNK

    @pl.when(i == 0)
    def _():
        carry_scr[...] = jnp.zeros(carry_scr.shape, F32)

    x = x_ref[...]
    h_scr[...] = _rms(x, g2_ref[...], NORM_EPS).astype(BF16)
    acc_scr[...] = jnp.zeros(acc_scr.shape, F32)
    seq_start = (i % tiles_per_seq) == 0
    row = lax.broadcasted_iota(jnp.int32, (tm, tf), 0)

    def chunk(j, carry):
        gu = jnp.dot(h_scr[...], wgu_ref[j], preferred_element_type=F32)
        gate, up = gu[:, :tf], gu[:, tf:]
        prev = jnp.where(seq_start, 0.0, carry_scr[j])
        carry_scr[j] = gate[tm - 8:, :]
        p1 = jnp.broadcast_to(prev[7:8], (tm, tf))
        p2 = jnp.broadcast_to(prev[6:7], (tm, tf))
        g1 = jnp.where(row >= 1, pltpu.roll(gate, 1, 0), p1)
        g2 = jnp.where(row >= 2, pltpu.roll(gate, 2, 0), jnp.where(row == 1, p1, p2))
        cw = cw_ref[j]
        conv = cw[2:3] * gate + cb_ref[j] + cw[1:2] * g1 + cw[0:1] * g2
        a = (_gelu_tanh(conv) * up).astype(BF16)
        acc_scr[...] += jnp.dot(a, wd_ref[j], preferred_element_type=F32)
        return carry

    lax.fori_loop(0, n_chunks, chunk, 0)
    out_ref[...] = x + _rms(acc_scr[...], g3_ref[...], NORM_EPS)


def _ffn(x, g2, g3, w_up, conv_w, conv_b, w_down, S):
    T, D = x.shape
    tm = ROW_TILE
    tf = FF_CHUNK
    nc = D_FF // tf
    wg = w_up[:, :D_FF].reshape(D, nc, tf)
    wu = w_up[:, D_FF:].reshape(D, nc, tf)
    wgu = jnp.concatenate([wg, wu], axis=2).transpose(1, 0, 2).astype(BF16)
    wd = w_down.reshape(nc, tf, D).astype(BF16)
    cw = conv_w.reshape(CONV_WIDTH, nc, tf).transpose(1, 0, 2)
    cb = conv_b.reshape(nc, 1, tf)
    kernel = functools.partial(_ffn_kernel, tiles_per_seq=S // tm, n_chunks=nc)
    return pl.pallas_call(
        kernel,
        grid=(T // tm,),
        in_specs=[
            pl.BlockSpec((tm, D), lambda i: (i, 0)),
            _resident((1, D)),
            _resident((1, D)),
            _resident((nc, D, 2 * tf)),
            _resident((nc, tf, D)),
            _resident((nc, CONV_WIDTH, tf)),
            _resident((nc, 1, tf)),
        ],
        out_specs=pl.BlockSpec((tm, D), lambda i: (i, 0)),
        out_shape=jax.ShapeDtypeStruct((T, D), F32),
        scratch_shapes=[
            pltpu.VMEM((tm, D), BF16),
            pltpu.VMEM((tm, D), F32),
            pltpu.VMEM((nc, 8, tf), F32),
        ],
        input_output_aliases={0: 0},
        compiler_params=_params(1),
        name="conv_ffn",
    )(x, g2.reshape(1, D), g3.reshape(1, D), wgu, wd, cw, cb)


def _dup_heads(w, n_heads):
    lead = w.shape[:-1]
    w = w.reshape(lead + (n_heads, 1, HEAD_DIM))
    return jnp.broadcast_to(w, lead + (n_heads, 2, HEAD_DIM)).reshape(lead + (2 * n_heads * HEAD_DIM,))


def kernel(x, norm_gains, w_qkv_a, lambda_qk_a, subln_a, w_o_a, kv_norm, w_kv_b, b_kv_b, w_q_b, b_q_b,
           sinks_b, w_o_b, w_up, conv_w, conv_b, w_down):
    B, S, D = x.shape
    T = B * S
    assert D == D_MODEL and S % ATTN_TILE == 0 and S % ROW_TILE == 0 and S >= 2 * WINDOW
    xt = x.reshape(T, D)
    kv3 = None
    for layer in range(DEPTH):
        g = norm_gains[layer]
        if layer < N_A_LAYERS:
            lambda_init = 0.8 - 0.6 * math.exp(-0.3 * layer)
            qkv3 = _proj(xt, g[0], w_qkv_a[layer].astype(BF16), jnp.zeros((3 * D,), F32))
            o = _attn_a(qkv3, lambda_qk_a[layer], subln_a[layer], lambda_init, B, S)
            xt = _post(o, w_o_a[layer].astype(BF16), xt, g[1])
        else:
            j = layer - N_A_LAYERS
            if kv3 is None:
                kvw = SWA_KV_HEADS * HEAD_DIM
                w_kv = jnp.concatenate([_dup_heads(w_kv_b[:, :kvw], SWA_KV_HEADS),
                                        _dup_heads(w_kv_b[:, kvw:], SWA_KV_HEADS)], axis=1)
                b_kv = jnp.concatenate([_dup_heads(b_kv_b[:kvw], SWA_KV_HEADS),
                                        _dup_heads(b_kv_b[kvw:], SWA_KV_HEADS)])
                kv3 = _proj(xt, kv_norm, w_kv.astype(BF16), b_kv)
            q3 = _proj(xt, g[0], w_q_b[j].astype(BF16), b_q_b[j])
            o = _attn_b(q3, kv3, sinks_b[j], B, S)
            xt = _post(o, w_o_b[j].astype(BF16), xt, g[1])
        xt = _ffn(xt, g[2], g[3], w_up[layer], conv_w[layer], conv_b[layer], w_down[layer], S)
    return xt.reshape(B, S, D)
```

```python
import functools
import math

import numpy as np
import jax
import jax.numpy as jnp
from jax import lax
from jax.experimental import pallas as pl
from jax.experimental.pallas import tpu as pltpu

D_MODEL = 1024
HEAD_DIM = 64
DEPTH = 4
N_A_LAYERS = DEPTH // 2
DIFF_HEADS = D_MODEL // (2 * HEAD_DIM)
SWA_Q_HEADS = D_MODEL // HEAD_DIM
SWA_KV_HEADS = 2
SWA_GROUP = SWA_Q_HEADS // SWA_KV_HEADS
WINDOW = 128
D_FF = 2816
CONV_WIDTH = 3
NORM_EPS = 1e-6
SUBLN_EPS = 1e-5

LANES = 128
QK_SCALE = HEAD_DIM ** -0.5
VMEM_LIMIT_BYTES = 56 * 1024 * 1024
ROW_TILE = 512
ATTN_TILE = 512
SUM_ROWS = 16
ATTN_QUERY_CHUNK = 512
FF_CHUNK = 256
FF_ROW_BLOCK = 128
POS_SPLIT = 64

BF16 = jnp.bfloat16
F32 = jnp.float32


def _alibi_slopes(n):
    return [2.0 ** (-8.0 * (i + 1) / n) for i in range(n)]


def _rms(x, g, eps):
    return x * lax.rsqrt(jnp.mean(x * x, axis=-1, keepdims=True) + eps) * g


def _params(n_axes):
    return pltpu.CompilerParams(
        dimension_semantics=("arbitrary",) * n_axes, vmem_limit_bytes=VMEM_LIMIT_BYTES)


def _resident(shape):
    nd = len(shape)
    return pl.BlockSpec(shape, lambda *_: (0,) * nd, pipeline_mode=pl.Buffered(1))


def _proj_kernel(x_ref, g_ref, w_ref, b_ref, o_ref, *, n_slabs, dot_slabs):
    h = _rms(x_ref[...], g_ref[...], NORM_EPS).astype(BF16)
    for c0 in range(0, n_slabs, dot_slabs):
        nc = min(dot_slabs, n_slabs - c0)
        cols = slice(c0 * LANES, (c0 + nc) * LANES)
        y = jnp.dot(h, w_ref[:, cols], preferred_element_type=F32) + b_ref[:, cols]
        for c in range(nc):
            o_ref[c0 + c] = y[:, c * LANES:(c + 1) * LANES].astype(BF16)


def _proj(x, g, w, b):
    T, D = x.shape
    N = w.shape[1]
    n_slabs = N // LANES
    tm = ROW_TILE
    return pl.pallas_call(
        functools.partial(_proj_kernel, n_slabs=n_slabs, dot_slabs=4),
        grid=(T // tm,),
        in_specs=[
            pl.BlockSpec((tm, D), lambda i: (i, 0)),
            _resident((1, D)),
            _resident((D, N)),
            _resident((1, N)),
        ],
        out_specs=pl.BlockSpec((n_slabs, tm, LANES), lambda i: (0, i, 0)),
        out_shape=jax.ShapeDtypeStruct((n_slabs, T, LANES), BF16),
        compiler_params=_params(1),
        name="norm_proj",
    )(x, g.reshape(1, D), w, b.reshape(1, N))


def _attn_a_kernel(q_ref, k_ref, v_ref, kpos_ref, lq_ref, sg_ref, o_ref,
                   kk_scr, vt_scr, qq_scr, s_scr, m_scr, acc_scr, *, tq, lambda_init):
    qi = pl.program_id(2)
    n_tiles = vt_scr.shape[0]
    cq = ATTN_QUERY_CHUNK
    n_chunks = 2 * tq // cq
    assert n_chunks % 2 == 0

    @pl.when(qi == 0)
    def _():
        kk_scr[:, :LANES] = k_ref[...]
        kk_scr[:, LANES:] = kpos_ref[...]
        for t in range(n_tiles):
            vt_scr[t, :LANES, :] = v_ref[t * tq:(t + 1) * tq, :].T
            vt_scr[t, LANES:, :] = jnp.ones((SUM_ROWS, tq), BF16)

    q = q_ref[...] * QK_SCALE
    lane = lax.broadcasted_iota(jnp.int32, (tq, LANES), 1)
    zero = jnp.zeros_like(q)
    ones_cols = jnp.where(lane < 2, 1.0, 0.0).astype(BF16)
    qq_scr[:tq, :LANES] = jnp.where(lane < HEAD_DIM, q, zero)
    qq_scr[tq:, :LANES] = jnp.where(lane >= HEAD_DIM, q, zero)
    qq_scr[:tq, LANES:] = ones_cols
    qq_scr[tq:, LANES:] = ones_cols

    m_scr[...] = jnp.full(m_scr.shape, -jnp.inf, F32)
    acc_scr[...] = jnp.zeros(acc_scr.shape, F32)

    def scores(ki, c):
        start = pl.multiple_of(ki * tq, tq)
        kk = kk_scr[pl.ds(start, tq), :]
        s_scr[c % 2] = lax.dot_general(kk, qq_scr[c * cq:(c + 1) * cq, :], (((1,), (1,)), ((), ())),
                                       preferred_element_type=F32)

    def softmax_pv(ki, c, diagonal):
        cols = slice(c * cq, (c + 1) * cq)
        s = s_scr[c % 2]
        if diagonal:
            key = lax.broadcasted_iota(jnp.int32, (tq, cq), 0)
            qry = lax.broadcasted_iota(jnp.int32, (tq, cq), 1) + (c * cq) % tq
            s = jnp.where(key <= qry, s, -jnp.inf)
        m_prev = m_scr[:, cols]
        m_new = jnp.maximum(m_prev, jnp.max(s, axis=0, keepdims=True))
        alpha = jnp.exp(m_prev - m_new)
        p = jnp.exp(s - m_new)
        acc_scr[:, cols] = alpha * acc_scr[:, cols] + jnp.dot(
            vt_scr[ki], p.astype(BF16), preferred_element_type=F32)
        m_scr[:, cols] = m_new

    scores(0, 0)

    def body(ki, carry):
        for c in range(n_chunks):
            if c + 1 < n_chunks:
                scores(ki, c + 1)
            else:
                scores(ki + 1, 0)
            softmax_pv(ki, c, False)
        return carry

    lax.fori_loop(0, qi, body, 0)
    for c in range(n_chunks):
        if c + 1 < n_chunks:
            scores(qi, c + 1)
        softmax_pv(qi, c, True)

    lq = lq_ref[...]
    lam = (jnp.exp(jnp.sum(lq[0:1] * lq[1:2], axis=-1, keepdims=True))
           - jnp.exp(jnp.sum(lq[2:3] * lq[3:4], axis=-1, keepdims=True)) + lambda_init)
    l = acc_scr[LANES:LANES + 1, :]
    o = (acc_scr[:LANES, :tq] / l[:, :tq] - lam * (acc_scr[:LANES, tq:] / l[:, tq:]))
    o = o * lax.rsqrt(jnp.mean(o * o, axis=0, keepdims=True) + SUBLN_EPS) * sg_ref[...]
    o_ref[...] = (o * (1.0 - lambda_init)).T.astype(BF16)


def _alibi_key_columns(S):
    assert S <= POS_SPLIT * 256
    j = np.arange(S)
    cols = np.zeros((DIFF_HEADS, S, LANES), np.float32)
    for h, slope in enumerate(_alibi_slopes(DIFF_HEADS)):
        cols[h, :, 0] = slope * POS_SPLIT * (j // POS_SPLIT)
        cols[h, :, 1] = slope * (j % POS_SPLIT)
    return jnp.asarray(cols, BF16)


def _attn_a(qkv3, lam_qk, subln_g, lambda_init, B, S):
    T = B * S
    H = DIFF_HEADS
    tq = ATTN_TILE
    nq = S // tq
    kernel = functools.partial(_attn_a_kernel, tq=tq, lambda_init=lambda_init)
    return pl.pallas_call(
        kernel,
        grid=(B, H, nq),
        in_specs=[
            pl.BlockSpec((None, tq, LANES), lambda b, h, qi: (h, b * nq + qi, 0)),
            pl.BlockSpec((None, S, LANES), lambda b, h, qi: (H + h, b, 0)),
            pl.BlockSpec((None, S, LANES), lambda b, h, qi: (2 * H + h, b, 0)),
            pl.BlockSpec((None, S, LANES), lambda b, h, qi: (h, 0, 0)),
            _resident((4, HEAD_DIM)),
            _resident((LANES, 1)),
        ],
        out_specs=pl.BlockSpec((tq, LANES), lambda b, h, qi: (b * nq + qi, h)),
        out_shape=jax.ShapeDtypeStruct((T, D_MODEL), BF16),
        scratch_shapes=[
            pltpu.VMEM((S, 2 * LANES), BF16),
            pltpu.VMEM((nq, LANES + SUM_ROWS, tq), BF16),
            pltpu.VMEM((2 * tq, 2 * LANES), BF16),
            pltpu.VMEM((2, tq, ATTN_QUERY_CHUNK), F32),
            pltpu.VMEM((1, 2 * tq), F32),
            pltpu.VMEM((LANES + SUM_ROWS, 2 * tq), F32),
        ],
        compiler_params=_params(3),
        name="diff_attn",
    )(qkv3, qkv3, qkv3, _alibi_key_columns(S), lam_qk, subln_g.reshape(LANES, 1))


def _attn_b_kernel(sinks_ref, q_ref, k_ref, v_ref, o_ref):
    n = pl.program_id(1)
    start = pl.multiple_of(jnp.maximum(n - 1, 0) * WINDOW, WINDOW)
    offset = n * WINDOW - start
    r = lax.broadcasted_iota(jnp.int32, (WINDOW, 2 * WINDOW), 0)
    c = lax.broadcasted_iota(jnp.int32, (WINDOW, 2 * WINDOW), 1)
    dist = offset + r - c
    valid = (dist >= 0) & (dist < WINDOW)
    distf = dist.astype(F32)
    lane = lax.broadcasted_iota(jnp.int32, (WINDOW, LANES), 1)
    slopes = _alibi_slopes(SWA_Q_HEADS)
    pairs_per_kv = SWA_GROUP // 2

    for kh in range(SWA_KV_HEADS):
        kwin = k_ref[kh, pl.ds(start, 2 * WINDOW), :]
        vwin = v_ref[kh, pl.ds(start, 2 * WINDOW), :]
        rows = []
        for cpair in range(pairs_per_kv):
            qc = q_ref[kh * pairs_per_kv + cpair] * QK_SCALE
            zero = jnp.zeros_like(qc)
            rows.append(jnp.where(lane < HEAD_DIM, qc, zero))
            rows.append(jnp.where(lane >= HEAD_DIM, qc, zero))
        s = lax.dot_general(jnp.concatenate(rows, axis=0), kwin, (((1,), (1,)), ((), ())),
                            preferred_element_type=F32)
        probs, denoms = [], []
        for g in range(SWA_GROUP):
            head = kh * SWA_GROUP + g
            sg = s[g * WINDOW:(g + 1) * WINDOW] - slopes[head] * distf
            sg = jnp.where(valid, sg, -jnp.inf)
            sink = sinks_ref[head]
            m = jnp.maximum(jnp.max(sg, axis=-1, keepdims=True), sink)
            e = jnp.exp(sg - m)
            denoms.append(jnp.sum(e, axis=-1, keepdims=True) + jnp.exp(sink - m))
            probs.append(e.astype(BF16))
        o = jnp.dot(jnp.concatenate(probs, axis=0), vwin, preferred_element_type=F32)
        for cpair in range(pairs_per_kv):
            ge, go = 2 * cpair, 2 * cpair + 1
            oe = o[ge * WINDOW:(ge + 1) * WINDOW] / denoms[ge]
            oo = o[go * WINDOW:(go + 1) * WINDOW] / denoms[go]
            col = (kh * pairs_per_kv + cpair) * LANES
            o_ref[:, col:col + LANES] = jnp.where(lane < HEAD_DIM, oe, oo).astype(BF16)


def _attn_b(q3, kv3, sinks, B, S):
    T = B * S
    nblk = S // WINDOW
    n_pairs = SWA_Q_HEADS // 2
    return pl.pallas_call(
        _attn_b_kernel,
        grid_spec=pltpu.PrefetchScalarGridSpec(
            num_scalar_prefetch=1,
            grid=(B, nblk),
            in_specs=[
                pl.BlockSpec((n_pairs, WINDOW, LANES), lambda b, n, *_: (0, b * nblk + n, 0)),
                pl.BlockSpec((SWA_KV_HEADS, S, LANES), lambda b, n, *_: (0, b, 0)),
                pl.BlockSpec((SWA_KV_HEADS, S, LANES), lambda b, n, *_: (1, b, 0)),
            ],
            out_specs=pl.BlockSpec((WINDOW, D_MODEL), lambda b, n, *_: (b * nblk + n, 0)),
        ),
        out_shape=jax.ShapeDtypeStruct((T, D_MODEL), BF16),
        compiler_params=_params(2),
        name="swa_attn",
    )(sinks, q3, kv3, kv3)


def _post_kernel(o_ref, w_ref, x_ref, g_ref, out_ref):
    mix = jnp.dot(o_ref[...], w_ref[...], preferred_element_type=F32)
    out_ref[...] = x_ref[...] + _rms(mix, g_ref[...], NORM_EPS)


def _post(o, w, x, g):
    T, D = x.shape
    tm = ROW_TILE
    return pl.pallas_call(
        _post_kernel,
        grid=(T // tm,),
        in_specs=[
            pl.BlockSpec((tm, D), lambda i: (i, 0)),
            _resident((D, D)),
            pl.BlockSpec((tm, D), lambda i: (i, 0)),
            _resident((1, D)),
        ],
        out_specs=pl.BlockSpec((tm, D), lambda i: (i, 0)),
        out_shape=jax.ShapeDtypeStruct((T, D), F32),
        input_output_aliases={2: 0},
        compiler_params=_params(1),
        name="out_proj_residual",
    )(o, w, x, g.reshape(1, D))


def _gelu_tanh(x):
    return 0.5 * x * (1.0 + jnp.tanh(math.sqrt(2.0 / math.pi) * (x + 0.044715 * (x * x * x))))


def _ffn_kernel(x_ref, g2_ref, g3_ref, wgu_ref, wd_ref, cw_ref, cb_ref, out_ref,
                h_scr, acc_scr, carry_scr, gate0_scr, gate1_scr, up0_scr, up1_scr,
                *, tiles_per_seq, n_chunks):
    i = pl.program_id(0)
    tm = x_ref.shape[0]
    tf = FF_CHUNK
    rb = FF_ROW_BLOCK
    slots = ((gate0_scr, up0_scr), (gate1_scr, up1_scr))

    @pl.when(i == 0)
    def _():
        carry_scr[...] = jnp.zeros(carry_scr.shape, F32)

    x = x_ref[...]
    h_scr[...] = _rms(x, g2_ref[...], NORM_EPS).astype(BF16)
    acc_scr[...] = jnp.zeros(acc_scr.shape, F32)
    seq_start = (i % tiles_per_seq) == 0

    def up_proj(j, slot, r):
        gate_scr, up_scr = slots[slot]
        if r == 0:
            gate_scr[0:8, :] = jnp.where(seq_start, 0.0, carry_scr[j])
        gu = jnp.dot(h_scr[r * rb:(r + 1) * rb, :], wgu_ref[j], preferred_element_type=F32)
        gate_scr[8 + r * rb:8 + (r + 1) * rb, :] = gu[:, :tf]
        up_scr[r * rb:(r + 1) * rb, :] = gu[:, tf:]

    def gate_and_down(j, slot, r):
        gate_scr, up_scr = slots[slot]
        lo = r * rb
        cw = cw_ref[j]
        conv = (cw[2:3] * gate_scr[8 + lo:8 + lo + rb, :] + cb_ref[j]
                + cw[1:2] * gate_scr[7 + lo:7 + lo + rb, :]
                + cw[0:1] * gate_scr[6 + lo:6 + lo + rb, :])
        a = (_gelu_tanh(conv) * up_scr[lo:lo + rb, :]).astype(BF16)
        acc_scr[lo:lo + rb, :] += jnp.dot(a, wd_ref[j], preferred_element_type=F32)
        if r == tm // rb - 1:
            carry_scr[j] = gate_scr[tm:tm + 8, :]

    def stage(j_next, slot_next, j_cur, slot_cur):
        for r in range(tm // rb):
            if j_next is not None:
                up_proj(j_next, slot_next, r)
            if j_cur is not None:
                gate_and_down(j_cur, slot_cur, r)

    assert n_chunks % 2 == 1
    stage(0, 0, None, None)

    def pair(k, carry):
        j = 2 * k
        stage(j + 1, 1, j, 0)
        stage(j + 2, 0, j + 1, 1)
        return carry

    lax.fori_loop(0, n_chunks // 2, pair, 0)
    stage(None, None, n_chunks - 1, 0)
    out_ref[...] = x + _rms(acc_scr[...], g3_ref[...], NORM_EPS)


def _ffn(x, g2, g3, w_up, conv_w, conv_b, w_down, S):
    T, D = x.shape
    tm = ROW_TILE
    tf = FF_CHUNK
    nc = D_FF // tf
    wg = w_up[:, :D_FF].reshape(D, nc, tf)
    wu = w_up[:, D_FF:].reshape(D, nc, tf)
    wgu = jnp.concatenate([wg, wu], axis=2).transpose(1, 0, 2).astype(BF16)
    wd = w_down.reshape(nc, tf, D).astype(BF16)
    cw = conv_w.reshape(CONV_WIDTH, nc, tf).transpose(1, 0, 2)
    cb = conv_b.reshape(nc, 1, tf)
    kernel = functools.partial(_ffn_kernel, tiles_per_seq=S // tm, n_chunks=nc)
    return pl.pallas_call(
        kernel,
        grid=(T // tm,),
        in_specs=[
            pl.BlockSpec((tm, D), lambda i: (i, 0)),
            _resident((1, D)),
            _resident((1, D)),
            _resident((nc, D, 2 * tf)),
            _resident((nc, tf, D)),
            _resident((nc, CONV_WIDTH, tf)),
            _resident((nc, 1, tf)),
        ],
        out_specs=pl.BlockSpec((tm, D), lambda i: (i, 0)),
        out_shape=jax.ShapeDtypeStruct((T, D), F32),
        scratch_shapes=[
            pltpu.VMEM((tm, D), BF16),
            pltpu.VMEM((tm, D), F32),
            pltpu.VMEM((nc, 8, tf), F32),
            pltpu.VMEM((tm + 8, tf), F32),
            pltpu.VMEM((tm + 8, tf), F32),
            pltpu.VMEM((tm, tf), F32),
            pltpu.VMEM((tm, tf), F32),
        ],
        input_output_aliases={0: 0},
        compiler_params=_params(1),
        name="conv_ffn",
    )(x, g2.reshape(1, D), g3.reshape(1, D), wgu, wd, cw, cb)


def _dup_heads(w, n_heads):
    lead = w.shape[:-1]
    w = w.reshape(lead + (n_heads, 1, HEAD_DIM))
    return jnp.broadcast_to(w, lead + (n_heads, 2, HEAD_DIM)).reshape(lead + (2 * n_heads * HEAD_DIM,))


def kernel(x, norm_gains, w_qkv_a, lambda_qk_a, subln_a, w_o_a, kv_norm, w_kv_b, b_kv_b, w_q_b, b_q_b,
           sinks_b, w_o_b, w_up, conv_w, conv_b, w_down):
    B, S, D = x.shape
    T = B * S
    assert D == D_MODEL and S % ATTN_TILE == 0 and S % ROW_TILE == 0 and S >= 2 * WINDOW
    xt = x.reshape(T, D)
    kv3 = None
    for layer in range(DEPTH):
        g = norm_gains[layer]
        if layer < N_A_LAYERS:
            lambda_init = 0.8 - 0.6 * math.exp(-0.3 * layer)
            qkv3 = _proj(xt, g[0], w_qkv_a[layer].astype(BF16), jnp.zeros((3 * D,), F32))
            o = _attn_a(qkv3, lambda_qk_a[layer], subln_a[layer], lambda_init, B, S)
            xt = _post(o, w_o_a[layer].astype(BF16), xt, g[1])
        else:
            j = layer - N_A_LAYERS
            if kv3 is None:
                kvw = SWA_KV_HEADS * HEAD_DIM
                w_kv = jnp.concatenate([_dup_heads(w_kv_b[:, :kvw], SWA_KV_HEADS),
                                        _dup_heads(w_kv_b[:, kvw:], SWA_KV_HEADS)], axis=1)
                b_kv = jnp.concatenate([_dup_heads(b_kv_b[:kvw], SWA_KV_HEADS),
                                        _dup_heads(b_kv_b[kvw:], SWA_KV_HEADS)])
                kv3 = _proj(xt, kv_norm, w_kv.astype(BF16), b_kv)
            q3 = _proj(xt, g[0], w_q_b[j].astype(BF16), b_q_b[j])
            o = _attn_b(q3, kv3, sinks_b[j], B, S)
            xt = _post(o, w_o_b[j].astype(BF16), xt, g[1])
        xt = _ffn(xt, g[2], g[3], w_up[layer], conv_w[layer], conv_b[layer], w_down[layer], S)
    return xt.reshape(B, S, D)
```

```python
import functools
import math

import ml_dtypes
import numpy as np
import jax
import jax.numpy as jnp
from jax import lax
from jax.experimental import pallas as pl
from jax.experimental.pallas import tpu as pltpu

D_MODEL = 1024
HEAD_DIM = 64
DEPTH = 4
N_A_LAYERS = DEPTH // 2
DIFF_HEADS = D_MODEL // (2 * HEAD_DIM)
SWA_Q_HEADS = D_MODEL // HEAD_DIM
SWA_KV_HEADS = 2
SWA_GROUP = SWA_Q_HEADS // SWA_KV_HEADS
WINDOW = 128
D_FF = 2816
CONV_WIDTH = 3
NORM_EPS = 1e-6
SUBLN_EPS = 1e-5

LANES = 128
QK_SCALE = HEAD_DIM ** -0.5
VMEM_LIMIT_BYTES = 56 * 1024 * 1024
ROW_TILE = 512
ATTN_QUERY_TILE = 1024
ATTN_KEY_TILE = 512
SUM_ROWS = 16
FF_CHUNK = 256
FF_ROW_BLOCK = 128
POS_PARTS = 3
LOG2E = math.log2(math.e)

BF16 = jnp.bfloat16
F32 = jnp.float32


def _alibi_slopes(n):
    return [2.0 ** (-8.0 * (i + 1) / n) for i in range(n)]


def _rms(x, g, eps):
    return x * lax.rsqrt(jnp.mean(x * x, axis=-1, keepdims=True) + eps) * g


def _params(n_axes):
    return pltpu.CompilerParams(
        dimension_semantics=("arbitrary",) * n_axes, vmem_limit_bytes=VMEM_LIMIT_BYTES)


def _resident(shape):
    nd = len(shape)
    return pl.BlockSpec(shape, lambda *_: (0,) * nd, pipeline_mode=pl.Buffered(1))


def _proj_kernel(x_ref, g_ref, w_ref, b_ref, o_ref, *, n_slabs, dot_slabs, n_scaled, scale):
    h = _rms(x_ref[...], g_ref[...], NORM_EPS).astype(BF16)
    for c0 in range(0, n_slabs, dot_slabs):
        nc = min(dot_slabs, n_slabs - c0)
        cols = slice(c0 * LANES, (c0 + nc) * LANES)
        y = jnp.dot(h, w_ref[:, cols], preferred_element_type=F32) + b_ref[:, cols]
        for c in range(nc):
            slab = y[:, c * LANES:(c + 1) * LANES]
            if c0 + c < n_scaled:
                slab = slab * scale
            o_ref[c0 + c] = slab.astype(BF16)


def _proj(x, g, w, b, n_scaled=0, scale=1.0):
    T, D = x.shape
    N = w.shape[1]
    n_slabs = N // LANES
    tm = ROW_TILE
    return pl.pallas_call(
        functools.partial(_proj_kernel, n_slabs=n_slabs, dot_slabs=4, n_scaled=n_scaled, scale=scale),
        grid=(T // tm,),
        in_specs=[
            pl.BlockSpec((tm, D), lambda i: (i, 0)),
            _resident((1, D)),
            _resident((D, N)),
            _resident((1, N)),
        ],
        out_specs=pl.BlockSpec((n_slabs, tm, LANES), lambda i: (0, i, 0)),
        out_shape=jax.ShapeDtypeStruct((n_slabs, T, LANES), BF16),
        compiler_params=_params(1),
        name="norm_proj",
    )(x, g.reshape(1, D), w, b.reshape(1, N))


def _attn_a_kernel(q_ref, k_ref, v_ref, kpos_ref, lq_ref, sg_ref, o_ref,
                   kk_scr, vt_scr, qq_scr, s_scr, m_scr, acc_scr, *, lambda_init):
    qi = pl.program_id(2)
    tq = q_ref.shape[0]
    tk = ATTN_KEY_TILE
    halves = tq // tk
    n_chunks = 2 * halves
    assert n_chunks % 2 == 0

    @pl.when(qi == 0)
    def _():
        kk_scr[:, :LANES] = k_ref[...]
        kk_scr[:, LANES:] = kpos_ref[...]
        for t in range(vt_scr.shape[0]):
            vt_scr[t, :LANES, :] = v_ref[t * tk:(t + 1) * tk, :].T
            vt_scr[t, LANES:, :] = jnp.ones((SUM_ROWS, tk), BF16)

    q = q_ref[...]
    lane = lax.broadcasted_iota(jnp.int32, (tq, LANES), 1)
    zero = jnp.zeros_like(q)
    ones_cols = jnp.where(lane < POS_PARTS, 1.0, 0.0).astype(BF16)
    qq_scr[:tq, :LANES] = jnp.where(lane < HEAD_DIM, q, zero)
    qq_scr[tq:, :LANES] = jnp.where(lane >= HEAD_DIM, q, zero)
    qq_scr[:tq, LANES:] = ones_cols
    qq_scr[tq:, LANES:] = ones_cols

    m_scr[...] = jnp.full(m_scr.shape, -jnp.inf, F32)
    acc_scr[...] = jnp.zeros(acc_scr.shape, F32)

    def scores(ki, c, slot):
        start = pl.multiple_of(ki * tk, tk)
        s_scr[slot] = lax.dot_general(
            kk_scr[pl.ds(start, tk), :], qq_scr[c * tk:(c + 1) * tk, :], (((1,), (1,)), ((), ())),
            preferred_element_type=F32)

    def softmax_pv(ki, c, slot, triangular):
        cols = slice(c * tk, (c + 1) * tk)
        s = s_scr[slot]
        if triangular:
            key = lax.broadcasted_iota(jnp.int32, (tk, tk), 0)
            qry = lax.broadcasted_iota(jnp.int32, (tk, tk), 1)
            s = jnp.where(key <= qry, s, -jnp.inf)
        m_prev = m_scr[:, cols]
        m_new = jnp.maximum(m_prev, jnp.max(s, axis=0, keepdims=True))
        alpha = jnp.exp2(m_prev - m_new)
        p = jnp.exp2(s - m_new)
        acc_scr[:, cols] = alpha * acc_scr[:, cols] + jnp.dot(
            vt_scr[ki], p.astype(BF16), preferred_element_type=F32)
        m_scr[:, cols] = m_new

    scores(0, 0, 0)

    def body(ki, carry):
        for c in range(n_chunks):
            if c + 1 < n_chunks:
                scores(ki, c + 1, (c + 1) % 2)
            else:
                scores(ki + 1, 0, 0)
            softmax_pv(ki, c, c % 2, False)
        return carry

    first_diag = halves * qi
    lax.fori_loop(0, first_diag, body, 0)

    items = [(d, c) for d in range(halves) for c in range(n_chunks) if c % halves >= d]
    for n, (d, c) in enumerate(items):
        if n + 1 < len(items):
            d_next, c_next = items[n + 1]
            scores(first_diag + d_next, c_next, (n + 1) % 2)
        softmax_pv(first_diag + d, c, n % 2, c % halves == d)

    lq = lq_ref[...]
    lam = (jnp.exp(jnp.sum(lq[0:1] * lq[1:2], axis=-1, keepdims=True))
           - jnp.exp(jnp.sum(lq[2:3] * lq[3:4], axis=-1, keepdims=True)) + lambda_init)
    l = acc_scr[LANES:LANES + 1, :]
    o = (acc_scr[:LANES, :tq] / l[:, :tq] - lam * (acc_scr[:LANES, tq:] / l[:, tq:]))
    o = o * lax.rsqrt(jnp.mean(o * o, axis=0, keepdims=True) + SUBLN_EPS) * sg_ref[...]
    o_ref[...] = (o * (1.0 - lambda_init)).T.astype(BF16)


def _alibi_key_columns(S):
    j = np.arange(S, dtype=np.float64)
    cols = np.zeros((DIFF_HEADS, S, LANES), np.float32)
    for h, slope in enumerate(_alibi_slopes(DIFF_HEADS)):
        rest = slope * LOG2E * j
        for part in range(POS_PARTS):
            piece = rest.astype(ml_dtypes.bfloat16).astype(np.float64)
            cols[h, :, part] = piece
            rest = rest - piece
    return jnp.asarray(cols, BF16)


def _attn_a(qkv3, lam_qk, subln_g, lambda_init, B, S):
    T = B * S
    H = DIFF_HEADS
    tq = ATTN_QUERY_TILE
    tk = ATTN_KEY_TILE
    nq = S // tq
    kernel = functools.partial(_attn_a_kernel, lambda_init=lambda_init)
    return pl.pallas_call(
        kernel,
        grid=(B, H, nq),
        in_specs=[
            pl.BlockSpec((None, tq, LANES), lambda b, h, qi: (h, b * nq + qi, 0)),
            pl.BlockSpec((None, S, LANES), lambda b, h, qi: (H + h, b, 0)),
            pl.BlockSpec((None, S, LANES), lambda b, h, qi: (2 * H + h, b, 0)),
            pl.BlockSpec((None, S, LANES), lambda b, h, qi: (h, 0, 0)),
            _resident((4, HEAD_DIM)),
            _resident((LANES, 1)),
        ],
        out_specs=pl.BlockSpec((tq, LANES), lambda b, h, qi: (b * nq + qi, h)),
        out_shape=jax.ShapeDtypeStruct((T, D_MODEL), BF16),
        scratch_shapes=[
            pltpu.VMEM((S, 2 * LANES), BF16),
            pltpu.VMEM((S // tk, LANES + SUM_ROWS, tk), BF16),
            pltpu.VMEM((2 * tq, 2 * LANES), BF16),
            pltpu.VMEM((2, tk, tk), F32),
            pltpu.VMEM((1, 2 * tq), F32),
            pltpu.VMEM((LANES + SUM_ROWS, 2 * tq), F32),
        ],
        compiler_params=_params(3),
        name="diff_attn",
    )(qkv3, qkv3, qkv3, _alibi_key_columns(S), lam_qk, subln_g.reshape(LANES, 1))


def _attn_b_kernel(sinks_ref, q_ref, k_ref, v_ref, o_ref):
    n = pl.program_id(1)
    start = pl.multiple_of(jnp.maximum(n - 1, 0) * WINDOW, WINDOW)
    offset = n * WINDOW - start
    r = lax.broadcasted_iota(jnp.int32, (WINDOW, 2 * WINDOW), 0)
    c = lax.broadcasted_iota(jnp.int32, (WINDOW, 2 * WINDOW), 1)
    dist = offset + r - c
    valid = (dist >= 0) & (dist < WINDOW)
    distf = dist.astype(F32)
    lane = lax.broadcasted_iota(jnp.int32, (WINDOW, LANES), 1)
    slopes = _alibi_slopes(SWA_Q_HEADS)
    pairs_per_kv = SWA_GROUP // 2

    for kh in range(SWA_KV_HEADS):
        kwin = k_ref[kh, pl.ds(start, 2 * WINDOW), :]
        vwin = v_ref[kh, pl.ds(start, 2 * WINDOW), :]
        rows = []
        for cpair in range(pairs_per_kv):
            qc = q_ref[kh * pairs_per_kv + cpair] * QK_SCALE
            zero = jnp.zeros_like(qc)
            rows.append(jnp.where(lane < HEAD_DIM, qc, zero))
            rows.append(jnp.where(lane >= HEAD_DIM, qc, zero))
        s = lax.dot_general(jnp.concatenate(rows, axis=0), kwin, (((1,), (1,)), ((), ())),
                            preferred_element_type=F32)
        probs, denoms = [], []
        for g in range(SWA_GROUP):
            head = kh * SWA_GROUP + g
            sg = s[g * WINDOW:(g + 1) * WINDOW] - slopes[head] * distf
            sg = jnp.where(valid, sg, -jnp.inf)
            sink = sinks_ref[head]
            m = jnp.maximum(jnp.max(sg, axis=-1, keepdims=True), sink)
            e = jnp.exp(sg - m)
            denoms.append(jnp.sum(e, axis=-1, keepdims=True) + jnp.exp(sink - m))
            probs.append(e.astype(BF16))
        o = jnp.dot(jnp.concatenate(probs, axis=0), vwin, preferred_element_type=F32)
        for cpair in range(pairs_per_kv):
            ge, go = 2 * cpair, 2 * cpair + 1
            oe = o[ge * WINDOW:(ge + 1) * WINDOW] / denoms[ge]
            oo = o[go * WINDOW:(go + 1) * WINDOW] / denoms[go]
            col = (kh * pairs_per_kv + cpair) * LANES
            o_ref[:, col:col + LANES] = jnp.where(lane < HEAD_DIM, oe, oo).astype(BF16)


def _attn_b(q3, kv3, sinks, B, S):
    T = B * S
    nblk = S // WINDOW
    n_pairs = SWA_Q_HEADS // 2
    return pl.pallas_call(
        _attn_b_kernel,
        grid_spec=pltpu.PrefetchScalarGridSpec(
            num_scalar_prefetch=1,
            grid=(B, nblk),
            in_specs=[
                pl.BlockSpec((n_pairs, WINDOW, LANES), lambda b, n, *_: (0, b * nblk + n, 0)),
                pl.BlockSpec((SWA_KV_HEADS, S, LANES), lambda b, n, *_: (0, b, 0)),
                pl.BlockSpec((SWA_KV_HEADS, S, LANES), lambda b, n, *_: (1, b, 0)),
            ],
            out_specs=pl.BlockSpec((WINDOW, D_MODEL), lambda b, n, *_: (b * nblk + n, 0)),
        ),
        out_shape=jax.ShapeDtypeStruct((T, D_MODEL), BF16),
        compiler_params=_params(2),
        name="swa_attn",
    )(sinks, q3, kv3, kv3)


def _post_kernel(o_ref, w_ref, x_ref, g_ref, out_ref):
    mix = jnp.dot(o_ref[...], w_ref[...], preferred_element_type=F32)
    out_ref[...] = x_ref[...] + _rms(mix, g_ref[...], NORM_EPS)


def _post(o, w, x, g):
    T, D = x.shape
    tm = ROW_TILE
    return pl.pallas_call(
        _post_kernel,
        grid=(T // tm,),
        in_specs=[
            pl.BlockSpec((tm, D), lambda i: (i, 0)),
            _resident((D, D)),
            pl.BlockSpec((tm, D), lambda i: (i, 0)),
            _resident((1, D)),
        ],
        out_specs=pl.BlockSpec((tm, D), lambda i: (i, 0)),
        out_shape=jax.ShapeDtypeStruct((T, D), F32),
        input_output_aliases={2: 0},
        compiler_params=_params(1),
        name="out_proj_residual",
    )(o, w, x, g.reshape(1, D))


def _gelu_tanh(x):
    return 0.5 * x * (1.0 + jnp.tanh(math.sqrt(2.0 / math.pi) * (x + 0.044715 * (x * x * x))))


def _ffn_kernel(x_ref, g2_ref, g3_ref, wgu_ref, wd_ref, cw_ref, cb_ref, out_ref,
                h_scr, acc_scr, carry_scr, gate0_scr, gate1_scr, up0_scr, up1_scr,
                *, tiles_per_seq, n_chunks):
    i = pl.program_id(0)
    tm = x_ref.shape[0]
    tf = FF_CHUNK
    rb = FF_ROW_BLOCK
    slots = ((gate0_scr, up0_scr), (gate1_scr, up1_scr))

    @pl.when(i == 0)
    def _():
        carry_scr[...] = jnp.zeros(carry_scr.shape, F32)

    x = x_ref[...]
    h_scr[...] = _rms(x, g2_ref[...], NORM_EPS).astype(BF16)
    acc_scr[...] = jnp.zeros(acc_scr.shape, F32)
    seq_start = (i % tiles_per_seq) == 0

    def up_proj(j, slot, r):
        gate_scr, up_scr = slots[slot]
        if r == 0:
            gate_scr[0:8, :] = jnp.where(seq_start, 0.0, carry_scr[j])
        gu = jnp.dot(h_scr[r * rb:(r + 1) * rb, :], wgu_ref[j], preferred_element_type=F32)
        gate_scr[8 + r * rb:8 + (r + 1) * rb, :] = gu[:, :tf]
        up_scr[r * rb:(r + 1) * rb, :] = gu[:, tf:]

    def gate_and_down(j, slot, r):
        gate_scr, up_scr = slots[slot]
        lo = r * rb
        cw = cw_ref[j]
        conv = (cw[2:3] * gate_scr[8 + lo:8 + lo + rb, :] + cb_ref[j]
                + cw[1:2] * gate_scr[7 + lo:7 + lo + rb, :]
                + cw[0:1] * gate_scr[6 + lo:6 + lo + rb, :])
        a = (_gelu_tanh(conv) * up_scr[lo:lo + rb, :]).astype(BF16)
        acc_scr[lo:lo + rb, :] += jnp.dot(a, wd_ref[j], preferred_element_type=F32)
        if r == tm // rb - 1:
            carry_scr[j] = gate_scr[tm:tm + 8, :]

    def stage(j_next, slot_next, j_cur, slot_cur):
        for r in range(tm // rb):
            if j_next is not None:
                up_proj(j_next, slot_next, r)
            if j_cur is not None:
                gate_and_down(j_cur, slot_cur, r)

    assert n_chunks % 2 == 1
    stage(0, 0, None, None)

    def pair(k, carry):
        j = 2 * k
        stage(j + 1, 1, j, 0)
        stage(j + 2, 0, j + 1, 1)
        return carry

    lax.fori_loop(0, n_chunks // 2, pair, 0)
    stage(None, None, n_chunks - 1, 0)
    out_ref[...] = x + _rms(acc_scr[...], g3_ref[...], NORM_EPS)


def _ffn(x, g2, g3, w_up, conv_w, conv_b, w_down, S):
    T, D = x.shape
    tm = ROW_TILE
    tf = FF_CHUNK
    nc = D_FF // tf
    wg = w_up[:, :D_FF].reshape(D, nc, tf)
    wu = w_up[:, D_FF:].reshape(D, nc, tf)
    wgu = jnp.concatenate([wg, wu], axis=2).transpose(1, 0, 2).astype(BF16)
    wd = w_down.reshape(nc, tf, D).astype(BF16)
    cw = conv_w.reshape(CONV_WIDTH, nc, tf).transpose(1, 0, 2)
    cb = conv_b.reshape(nc, 1, tf)
    kernel = functools.partial(_ffn_kernel, tiles_per_seq=S // tm, n_chunks=nc)
    return pl.pallas_call(
        kernel,
        grid=(T // tm,),
        in_specs=[
            pl.BlockSpec((tm, D), lambda i: (i, 0)),
            _resident((1, D)),
            _resident((1, D)),
            _resident((nc, D, 2 * tf)),
            _resident((nc, tf, D)),
            _resident((nc, CONV_WIDTH, tf)),
            _resident((nc, 1, tf)),
        ],
        out_specs=pl.BlockSpec((tm, D), lambda i: (i, 0)),
        out_shape=jax.ShapeDtypeStruct((T, D), F32),
        scratch_shapes=[
            pltpu.VMEM((tm, D), BF16),
            pltpu.VMEM((tm, D), F32),
            pltpu.VMEM((nc, 8, tf), F32),
            pltpu.VMEM((tm + 8, tf), F32),
            pltpu.VMEM((tm + 8, tf), F32),
            pltpu.VMEM((tm, tf), F32),
            pltpu.VMEM((tm, tf), F32),
        ],
        input_output_aliases={0: 0},
        compiler_params=_params(1),
        name="conv_ffn",
    )(x, g2.reshape(1, D), g3.reshape(1, D), wgu, wd, cw, cb)


def _dup_heads(w, n_heads):
    lead = w.shape[:-1]
    w = w.reshape(lead + (n_heads, 1, HEAD_DIM))
    return jnp.broadcast_to(w, lead + (n_heads, 2, HEAD_DIM)).reshape(lead + (2 * n_heads * HEAD_DIM,))


def kernel(x, norm_gains, w_qkv_a, lambda_qk_a, subln_a, w_o_a, kv_norm, w_kv_b, b_kv_b, w_q_b, b_q_b,
           sinks_b, w_o_b, w_up, conv_w, conv_b, w_down):
    B, S, D = x.shape
    T = B * S
    assert D == D_MODEL and S % ATTN_QUERY_TILE == 0 and S % ROW_TILE == 0 and S >= 2 * WINDOW
    xt = x.reshape(T, D)
    kv3 = None
    for layer in range(DEPTH):
        g = norm_gains[layer]
        if layer < N_A_LAYERS:
            lambda_init = 0.8 - 0.6 * math.exp(-0.3 * layer)
            qkv3 = _proj(xt, g[0], w_qkv_a[layer].astype(BF16), jnp.zeros((3 * D,), F32),
                         n_scaled=DIFF_HEADS, scale=QK_SCALE * LOG2E)
            o = _attn_a(qkv3, lambda_qk_a[layer], subln_a[layer], lambda_init, B, S)
            xt = _post(o, w_o_a[layer].astype(BF16), xt, g[1])
        else:
            j = layer - N_A_LAYERS
            if kv3 is None:
                kvw = SWA_KV_HEADS * HEAD_DIM
                w_kv = jnp.concatenate([_dup_heads(w_kv_b[:, :kvw], SWA_KV_HEADS),
                                        _dup_heads(w_kv_b[:, kvw:], SWA_KV_HEADS)], axis=1)
                b_kv = jnp.concatenate([_dup_heads(b_kv_b[:kvw], SWA_KV_HEADS),
                                        _dup_heads(b_kv_b[kvw:], SWA_KV_HEADS)])
                kv3 = _proj(xt, kv_norm, w_kv.astype(BF16), b_kv)
            q3 = _proj(xt, g[0], w_q_b[j].astype(BF16), b_q_b[j])
            o = _attn_b(q3, kv3, sinks_b[j], B, S)
            xt = _post(o, w_o_b[j].astype(BF16), xt, g[1])
        xt = _ffn(xt, g[2], g[3], w_up[layer], conv_w[layer], conv_b[layer], w_down[layer], S)
    return xt.reshape(B, S, D)
```

```python
import functools
import math

import ml_dtypes
import numpy as np
import jax
import jax.numpy as jnp
from jax import lax
from jax.experimental import pallas as pl
from jax.experimental.pallas import tpu as pltpu

D_MODEL = 1024
HEAD_DIM = 64
DEPTH = 4
N_A_LAYERS = DEPTH // 2
DIFF_HEADS = D_MODEL // (2 * HEAD_DIM)
SWA_Q_HEADS = D_MODEL // HEAD_DIM
SWA_KV_HEADS = 2
SWA_GROUP = SWA_Q_HEADS // SWA_KV_HEADS
WINDOW = 128
D_FF = 2816
CONV_WIDTH = 3
NORM_EPS = 1e-6
SUBLN_EPS = 1e-5

LANES = 128
QK_SCALE = HEAD_DIM ** -0.5
VMEM_LIMIT_BYTES = 56 * 1024 * 1024
ROW_TILE = 512
ATTN_QUERY_TILE = 2048
ATTN_KEY_TILE = 512
SUM_ROWS = 16
SWA_BLOCKS_PER_STEP = 8
FF_CHUNK = 256
FF_ROW_BLOCK = 128
POS_PARTS = 3
LOG2E = math.log2(math.e)

BF16 = jnp.bfloat16
F32 = jnp.float32


def _alibi_slopes(n):
    return [2.0 ** (-8.0 * (i + 1) / n) for i in range(n)]


def _rms(x, g, eps):
    return x * lax.rsqrt(jnp.mean(x * x, axis=-1, keepdims=True) + eps) * g


def _params(n_axes):
    return pltpu.CompilerParams(
        dimension_semantics=("arbitrary",) * n_axes, vmem_limit_bytes=VMEM_LIMIT_BYTES)


def _resident(shape):
    nd = len(shape)
    return pl.BlockSpec(shape, lambda *_: (0,) * nd, pipeline_mode=pl.Buffered(1))


def _proj_kernel(x_ref, g_ref, w_ref, b_ref, o_ref, *, n_slabs, dot_slabs, n_scaled, scale):
    h = _rms(x_ref[...], g_ref[...], NORM_EPS).astype(BF16)
    for c0 in range(0, n_slabs, dot_slabs):
        nc = min(dot_slabs, n_slabs - c0)
        cols = slice(c0 * LANES, (c0 + nc) * LANES)
        y = jnp.dot(h, w_ref[:, cols], preferred_element_type=F32) + b_ref[:, cols]
        for c in range(nc):
            slab = y[:, c * LANES:(c + 1) * LANES]
            if c0 + c < n_scaled:
                slab = slab * scale
            o_ref[c0 + c] = slab.astype(BF16)


def _proj(x, g, w, b, n_scaled=0, scale=1.0):
    T, D = x.shape
    N = w.shape[1]
    n_slabs = N // LANES
    tm = ROW_TILE
    return pl.pallas_call(
        functools.partial(_proj_kernel, n_slabs=n_slabs, dot_slabs=4, n_scaled=n_scaled, scale=scale),
        grid=(T // tm,),
        in_specs=[
            pl.BlockSpec((tm, D), lambda i: (i, 0)),
            _resident((1, D)),
            _resident((D, N)),
            _resident((1, N)),
        ],
        out_specs=pl.BlockSpec((n_slabs, tm, LANES), lambda i: (0, i, 0)),
        out_shape=jax.ShapeDtypeStruct((n_slabs, T, LANES), BF16),
        compiler_params=_params(1),
        name="norm_proj",
    )(x, g.reshape(1, D), w, b.reshape(1, N))


def _attn_a_kernel(q_ref, k_ref, v_ref, kpos_ref, lq_ref, sg_ref, o_ref,
                   kk_scr, vt_scr, qq_scr, s_scr, m_scr, acc_scr, *, lambda_init):
    qi = pl.program_id(2)
    tq = q_ref.shape[0]
    tk = ATTN_KEY_TILE
    halves = tq // tk
    n_chunks = 2 * halves
    assert n_chunks % 2 == 0

    @pl.when(qi == 0)
    def _():
        kk_scr[:, :LANES] = k_ref[...]
        kk_scr[:, LANES:] = kpos_ref[...]
        for t in range(vt_scr.shape[0]):
            vt_scr[t, :LANES, :] = v_ref[t * tk:(t + 1) * tk, :].T
            vt_scr[t, LANES:, :] = jnp.ones((SUM_ROWS, tk), BF16)

    q = q_ref[...]
    lane = lax.broadcasted_iota(jnp.int32, (tq, LANES), 1)
    zero = jnp.zeros_like(q)
    ones_cols = jnp.where(lane < POS_PARTS, 1.0, 0.0).astype(BF16)
    qq_scr[:tq, :LANES] = jnp.where(lane < HEAD_DIM, q, zero)
    qq_scr[tq:, :LANES] = jnp.where(lane >= HEAD_DIM, q, zero)
    qq_scr[:tq, LANES:] = ones_cols
    qq_scr[tq:, LANES:] = ones_cols

    m_scr[...] = jnp.full(m_scr.shape, -jnp.inf, F32)
    acc_scr[...] = jnp.zeros(acc_scr.shape, F32)

    def scores(ki, c, slot):
        start = pl.multiple_of(ki * tk, tk)
        s_scr[slot] = lax.dot_general(
            kk_scr[pl.ds(start, tk), :], qq_scr[c * tk:(c + 1) * tk, :], (((1,), (1,)), ((), ())),
            preferred_element_type=F32)

    def softmax_pv(ki, c, slot, triangular):
        cols = slice(c * tk, (c + 1) * tk)
        s = s_scr[slot]
        if triangular:
            key = lax.broadcasted_iota(jnp.int32, (tk, tk), 0)
            qry = lax.broadcasted_iota(jnp.int32, (tk, tk), 1)
            s = jnp.where(key <= qry, s, -jnp.inf)
        m_prev = m_scr[:, cols]
        m_new = jnp.maximum(m_prev, jnp.max(s, axis=0, keepdims=True))
        alpha = jnp.exp2(m_prev - m_new)
        p = jnp.exp2(s - m_new)
        acc_scr[:, cols] = alpha * acc_scr[:, cols] + jnp.dot(
            vt_scr[ki], p.astype(BF16), preferred_element_type=F32)
        m_scr[:, cols] = m_new

    scores(0, 0, 0)

    def body(ki, carry):
        for c in range(n_chunks):
            if c + 1 < n_chunks:
                scores(ki, c + 1, (c + 1) % 2)
            else:
                scores(ki + 1, 0, 0)
            softmax_pv(ki, c, c % 2, False)
        return carry

    first_diag = halves * qi
    lax.fori_loop(0, first_diag, body, 0)

    items = [(d, c) for d in range(halves) for c in range(n_chunks) if c % halves >= d]
    for n, (d, c) in enumerate(items):
        if n + 1 < len(items):
            d_next, c_next = items[n + 1]
            scores(first_diag + d_next, c_next, (n + 1) % 2)
        softmax_pv(first_diag + d, c, n % 2, c % halves == d)

    lq = lq_ref[...]
    lam = (jnp.exp(jnp.sum(lq[0:1] * lq[1:2], axis=-1, keepdims=True))
           - jnp.exp(jnp.sum(lq[2:3] * lq[3:4], axis=-1, keepdims=True)) + lambda_init)
    l = acc_scr[LANES:LANES + 1, :]
    o = (acc_scr[:LANES, :tq] / l[:, :tq] - lam * (acc_scr[:LANES, tq:] / l[:, tq:]))
    o = o * lax.rsqrt(jnp.mean(o * o, axis=0, keepdims=True) + SUBLN_EPS) * sg_ref[...]
    o_ref[...] = (o * (1.0 - lambda_init)).T.astype(BF16)


def _alibi_key_columns(S):
    j = np.arange(S, dtype=np.float64)
    cols = np.zeros((DIFF_HEADS, S, LANES), np.float32)
    for h, slope in enumerate(_alibi_slopes(DIFF_HEADS)):
        rest = slope * LOG2E * j
        for part in range(POS_PARTS):
            piece = rest.astype(ml_dtypes.bfloat16).astype(np.float64)
            cols[h, :, part] = piece
            rest = rest - piece
    return jnp.asarray(cols, BF16)


def _attn_a(qkv3, lam_qk, subln_g, lambda_init, B, S):
    T = B * S
    H = DIFF_HEADS
    tq = ATTN_QUERY_TILE
    tk = ATTN_KEY_TILE
    nq = S // tq
    kernel = functools.partial(_attn_a_kernel, lambda_init=lambda_init)
    return pl.pallas_call(
        kernel,
        grid=(B, H, nq),
        in_specs=[
            pl.BlockSpec((None, tq, LANES), lambda b, h, qi: (h, b * nq + qi, 0)),
            pl.BlockSpec((None, S, LANES), lambda b, h, qi: (H + h, b, 0)),
            pl.BlockSpec((None, S, LANES), lambda b, h, qi: (2 * H + h, b, 0)),
            pl.BlockSpec((None, S, LANES), lambda b, h, qi: (h, 0, 0)),
            _resident((4, HEAD_DIM)),
            _resident((LANES, 1)),
        ],
        out_specs=pl.BlockSpec((tq, LANES), lambda b, h, qi: (b * nq + qi, h)),
        out_shape=jax.ShapeDtypeStruct((T, D_MODEL), BF16),
        scratch_shapes=[
            pltpu.VMEM((S, 2 * LANES), BF16),
            pltpu.VMEM((S // tk, LANES + SUM_ROWS, tk), BF16),
            pltpu.VMEM((2 * tq, 2 * LANES), BF16),
            pltpu.VMEM((2, tk, tk), F32),
            pltpu.VMEM((1, 2 * tq), F32),
            pltpu.VMEM((LANES + SUM_ROWS, 2 * tq), F32),
        ],
        compiler_params=_params(3),
        name="diff_attn",
    )(qkv3, qkv3, qkv3, _alibi_key_columns(S), lam_qk, subln_g.reshape(LANES, 1))


def _attn_b_kernel(sinks_ref, q_ref, k_ref, v_ref, bias_ref, o_ref):
    lane = lax.broadcasted_iota(jnp.int32, (WINDOW, LANES), 1)
    pairs_per_kv = SWA_GROUP // 2
    group_lanes = SWA_GROUP * WINDOW
    ones_rows = jnp.ones((SUM_ROWS, 2 * WINDOW), BF16)

    def window_start(qb):
        nb = pl.program_id(1) * SWA_BLOCKS_PER_STEP + qb
        return nb, pl.multiple_of(jnp.maximum(nb - 1, 0) * WINDOW, WINDOW)

    def scores(qb, kh):
        _, start = window_start(qb)
        kwin = k_ref[kh, pl.ds(start, 2 * WINDOW), :]
        rows = []
        for cpair in range(pairs_per_kv):
            qc = q_ref[kh * pairs_per_kv + cpair, qb * WINDOW:(qb + 1) * WINDOW, :]
            zero = jnp.zeros_like(qc)
            rows.append(jnp.where(lane < HEAD_DIM, qc, zero))
            rows.append(jnp.where(lane >= HEAD_DIM, qc, zero))
        return lax.dot_general(kwin, jnp.concatenate(rows, axis=0), (((1,), (1,)), ((), ())),
                               preferred_element_type=F32)

    def softmax_pv(qb, kh, s):
        nb, start = window_start(qb)
        bias = bias_ref.at[jnp.minimum(nb, 1) if qb == 0 else 1]
        s = s + bias[:, kh * group_lanes:(kh + 1) * group_lanes]
        sink = jnp.concatenate(
            [jnp.full((1, WINDOW), sinks_ref[kh * SWA_GROUP + g] * LOG2E, F32) for g in range(SWA_GROUP)],
            axis=1)
        m = jnp.maximum(jnp.max(s, axis=0, keepdims=True), sink)
        p = jnp.exp2(s - m).astype(BF16)
        vt = v_ref[pl.ds(start, 2 * WINDOW), :].T[kh * HEAD_DIM:(kh + 1) * HEAD_DIM]
        o = jnp.dot(jnp.concatenate([vt, ones_rows], axis=0), p,
                    preferred_element_type=F32)
        denom = o[HEAD_DIM:HEAD_DIM + 1] + jnp.exp2(sink - m)
        o = o[:HEAD_DIM] * (1.0 / denom)
        for cpair in range(pairs_per_kv):
            pair = jnp.concatenate([o[:, (2 * cpair) * WINDOW:(2 * cpair + 1) * WINDOW],
                                    o[:, (2 * cpair + 1) * WINDOW:(2 * cpair + 2) * WINDOW]], axis=0)
            col = (kh * pairs_per_kv + cpair) * LANES
            o_ref[qb * WINDOW:(qb + 1) * WINDOW, col:col + LANES] = pair.T.astype(BF16)

    items = [(qb, kh) for qb in range(SWA_BLOCKS_PER_STEP) for kh in range(SWA_KV_HEADS)]
    s_next = scores(*items[0])
    for n, item in enumerate(items):
        s_cur = s_next
        if n + 1 < len(items):
            s_next = scores(*items[n + 1])
        softmax_pv(*item, s_cur)


def _swa_bias():
    key = np.arange(2 * WINDOW)[:, None]
    qry = np.arange(WINDOW)[None, :]
    bias = np.full((2, 2 * WINDOW, SWA_Q_HEADS * WINDOW), -np.inf, np.float32)
    for case, offset in enumerate((0, WINDOW)):
        dist = offset + qry - key
        valid = (dist >= 0) & (dist < WINDOW)
        for h, slope in enumerate(_alibi_slopes(SWA_Q_HEADS)):
            bias[case, :, h * WINDOW:(h + 1) * WINDOW] = np.where(valid, -slope * LOG2E * dist, -np.inf)
    return jnp.asarray(bias)


def _attn_b(q3, k3, v3, sinks, B, S):
    T = B * S
    rows = SWA_BLOCKS_PER_STEP * WINDOW
    nstep = S // rows
    n_pairs = SWA_Q_HEADS // 2
    return pl.pallas_call(
        _attn_b_kernel,
        grid_spec=pltpu.PrefetchScalarGridSpec(
            num_scalar_prefetch=1,
            grid=(B, nstep),
            in_specs=[
                pl.BlockSpec((n_pairs, rows, LANES), lambda b, n, *_: (0, b * nstep + n, 0)),
                pl.BlockSpec((SWA_KV_HEADS, S, LANES), lambda b, n, *_: (0, b, 0)),
                pl.BlockSpec((None, S, LANES), lambda b, n, *_: (SWA_KV_HEADS, b, 0)),
                _resident((2, 2 * WINDOW, SWA_Q_HEADS * WINDOW)),
            ],
            out_specs=pl.BlockSpec((rows, D_MODEL), lambda b, n, *_: (b * nstep + n, 0)),
        ),
        out_shape=jax.ShapeDtypeStruct((T, D_MODEL), BF16),
        compiler_params=_params(2),
        name="swa_attn",
    )(sinks, q3, k3, v3, _swa_bias())


def _post_kernel(o_ref, w_ref, x_ref, g_ref, out_ref):
    mix = jnp.dot(o_ref[...], w_ref[...], preferred_element_type=F32)
    out_ref[...] = x_ref[...] + _rms(mix, g_ref[...], NORM_EPS)


def _post(o, w, x, g):
    T, D = x.shape
    tm = ROW_TILE
    return pl.pallas_call(
        _post_kernel,
        grid=(T // tm,),
        in_specs=[
            pl.BlockSpec((tm, D), lambda i: (i, 0)),
            _resident((D, D)),
            pl.BlockSpec((tm, D), lambda i: (i, 0)),
            _resident((1, D)),
        ],
        out_specs=pl.BlockSpec((tm, D), lambda i: (i, 0)),
        out_shape=jax.ShapeDtypeStruct((T, D), F32),
        input_output_aliases={2: 0},
        compiler_params=_params(1),
        name="out_proj_residual",
    )(o, w, x, g.reshape(1, D))


def _gelu_tanh(x):
    return 0.5 * x * (1.0 + jnp.tanh(math.sqrt(2.0 / math.pi) * (x + 0.044715 * (x * x * x))))


def _ffn_kernel(x_ref, g2_ref, g3_ref, wgu_ref, wd_ref, cw_ref, cb_ref, out_ref,
                h_scr, acc_scr, carry_scr, gate0_scr, gate1_scr, up0_scr, up1_scr,
                *, tiles_per_seq, n_chunks):
    i = pl.program_id(0)
    tm = x_ref.shape[0]
    tf = FF_CHUNK
    rb = FF_ROW_BLOCK
    slots = ((gate0_scr, up0_scr), (gate1_scr, up1_scr))

    @pl.when(i == 0)
    def _():
        carry_scr[...] = jnp.zeros(carry_scr.shape, F32)

    x = x_ref[...]
    h_scr[...] = _rms(x, g2_ref[...], NORM_EPS).astype(BF16)
    acc_scr[...] = jnp.zeros(acc_scr.shape, F32)
    seq_start = (i % tiles_per_seq) == 0

    def up_proj(j, slot, r):
        gate_scr, up_scr = slots[slot]
        if r == 0:
            gate_scr[0:8, :] = jnp.where(seq_start, 0.0, carry_scr[j])
        gu = jnp.dot(h_scr[r * rb:(r + 1) * rb, :], wgu_ref[j], preferred_element_type=F32)
        gate_scr[8 + r * rb:8 + (r + 1) * rb, :] = gu[:, :tf]
        up_scr[r * rb:(r + 1) * rb, :] = gu[:, tf:]

    def gate_and_down(j, slot, r):
        gate_scr, up_scr = slots[slot]
        lo = r * rb
        cw = cw_ref[j]
        conv = (cw[2:3] * gate_scr[8 + lo:8 + lo + rb, :] + cb_ref[j]
                + cw[1:2] * gate_scr[7 + lo:7 + lo + rb, :]
                + cw[0:1] * gate_scr[6 + lo:6 + lo + rb, :])
        a = (_gelu_tanh(conv) * up_scr[lo:lo + rb, :]).astype(BF16)
        acc_scr[lo:lo + rb, :] += jnp.dot(a, wd_ref[j], preferred_element_type=F32)
        if r == tm // rb - 1:
            carry_scr[j] = gate_scr[tm:tm + 8, :]

    def stage(j_next, slot_next, j_cur, slot_cur):
        for r in range(tm // rb):
            if j_next is not None:
                up_proj(j_next, slot_next, r)
            if j_cur is not None:
                gate_and_down(j_cur, slot_cur, r)

    assert n_chunks % 2 == 1
    stage(0, 0, None, None)

    def pair(k, carry):
        j = 2 * k
        stage(j + 1, 1, j, 0)
        stage(j + 2, 0, j + 1, 1)
        return carry

    lax.fori_loop(0, n_chunks // 2, pair, 0)
    stage(None, None, n_chunks - 1, 0)
    out_ref[...] = x + _rms(acc_scr[...], g3_ref[...], NORM_EPS)


def _ffn(x, g2, g3, w_up, conv_w, conv_b, w_down, S):
    T, D = x.shape
    tm = ROW_TILE
    tf = FF_CHUNK
    nc = D_FF // tf
    wg = w_up[:, :D_FF].reshape(D, nc, tf)
    wu = w_up[:, D_FF:].reshape(D, nc, tf)
    wgu = jnp.concatenate([wg, wu], axis=2).transpose(1, 0, 2).astype(BF16)
    wd = w_down.reshape(nc, tf, D).astype(BF16)
    cw = conv_w.reshape(CONV_WIDTH, nc, tf).transpose(1, 0, 2)
    cb = conv_b.reshape(nc, 1, tf)
    kernel = functools.partial(_ffn_kernel, tiles_per_seq=S // tm, n_chunks=nc)
    return pl.pallas_call(
        kernel,
        grid=(T // tm,),
        in_specs=[
            pl.BlockSpec((tm, D), lambda i: (i, 0)),
            _resident((1, D)),
            _resident((1, D)),
            _resident((nc, D, 2 * tf)),
            _resident((nc, tf, D)),
            _resident((nc, CONV_WIDTH, tf)),
            _resident((nc, 1, tf)),
        ],
        out_specs=pl.BlockSpec((tm, D), lambda i: (i, 0)),
        out_shape=jax.ShapeDtypeStruct((T, D), F32),
        scratch_shapes=[
            pltpu.VMEM((tm, D), BF16),
            pltpu.VMEM((tm, D), F32),
            pltpu.VMEM((nc, 8, tf), F32),
            pltpu.VMEM((tm + 8, tf), F32),
            pltpu.VMEM((tm + 8, tf), F32),
            pltpu.VMEM((tm, tf), F32),
            pltpu.VMEM((tm, tf), F32),
        ],
        input_output_aliases={0: 0},
        compiler_params=_params(1),
        name="conv_ffn",
    )(x, g2.reshape(1, D), g3.reshape(1, D), wgu, wd, cw, cb)


def _dup_heads(w, n_heads):
    lead = w.shape[:-1]
    w = w.reshape(lead + (n_heads, 1, HEAD_DIM))
    return jnp.broadcast_to(w, lead + (n_heads, 2, HEAD_DIM)).reshape(lead + (2 * n_heads * HEAD_DIM,))


def kernel(x, norm_gains, w_qkv_a, lambda_qk_a, subln_a, w_o_a, kv_norm, w_kv_b, b_kv_b, w_q_b, b_q_b,
           sinks_b, w_o_b, w_up, conv_w, conv_b, w_down):
    B, S, D = x.shape
    T = B * S
    assert D == D_MODEL and S % ATTN_QUERY_TILE == 0 and S % ROW_TILE == 0 and S >= 2 * WINDOW
    xt = x.reshape(T, D)
    kv3 = None
    for layer in range(DEPTH):
        g = norm_gains[layer]
        if layer < N_A_LAYERS:
            lambda_init = 0.8 - 0.6 * math.exp(-0.3 * layer)
            qkv3 = _proj(xt, g[0], w_qkv_a[layer].astype(BF16), jnp.zeros((3 * D,), F32),
                         n_scaled=DIFF_HEADS, scale=QK_SCALE * LOG2E)
            o = _attn_a(qkv3, lambda_qk_a[layer], subln_a[layer], lambda_init, B, S)
            xt = _post(o, w_o_a[layer].astype(BF16), xt, g[1])
        else:
            j = layer - N_A_LAYERS
            if kv3 is None:
                kvw = SWA_KV_HEADS * HEAD_DIM
                w_kv = jnp.concatenate([_dup_heads(w_kv_b[:, :kvw], SWA_KV_HEADS), w_kv_b[:, kvw:]], axis=1)
                b_kv = jnp.concatenate([_dup_heads(b_kv_b[:kvw], SWA_KV_HEADS), b_kv_b[kvw:]])
                kv3 = _proj(xt, kv_norm, w_kv.astype(BF16), b_kv)
            q3 = _proj(xt, g[0], w_q_b[j].astype(BF16), b_q_b[j],
                       n_scaled=SWA_Q_HEADS // 2, scale=QK_SCALE * LOG2E)
            o = _attn_b(q3, kv3, kv3, sinks_b[j], B, S)
            xt = _post(o, w_o_b[j].astype(BF16), xt, g[1])
        xt = _ffn(xt, g[2], g[3], w_up[layer], conv_w[layer], conv_b[layer], w_down[layer], S)
    return xt.reshape(B, S, D)
```

```python
import functools
import math

import ml_dtypes
import numpy as np
import jax
import jax.numpy as jnp
from jax import lax
from jax.experimental import pallas as pl
from jax.experimental.pallas import tpu as pltpu

D_MODEL = 1024
HEAD_DIM = 64
DEPTH = 4
N_A_LAYERS = DEPTH // 2
DIFF_HEADS = D_MODEL // (2 * HEAD_DIM)
SWA_Q_HEADS = D_MODEL // HEAD_DIM
SWA_KV_HEADS = 2
SWA_GROUP = SWA_Q_HEADS // SWA_KV_HEADS
WINDOW = 128
D_FF = 2816
CONV_WIDTH = 3
NORM_EPS = 1e-6
SUBLN_EPS = 1e-5

LANES = 128
QK_SCALE = HEAD_DIM ** -0.5
VMEM_LIMIT_BYTES = 56 * 1024 * 1024
ROW_TILE = 512
ATTN_QUERY_TILE = 2048
ATTN_KEY_TILE = 512
SUM_ROWS = 16
SWA_BLOCKS_PER_STEP = 8
FF_CHUNK = 256
FF_DOT_COLS = 512
FF_ROW_TILE = 512
FF_ROW_BLOCK = 256
POS_PARTS = 3
LOG2E = math.log2(math.e)

BF16 = jnp.bfloat16
F32 = jnp.float32


def _alibi_slopes(n):
    return [2.0 ** (-8.0 * (i + 1) / n) for i in range(n)]


def _rms(x, g, eps):
    return x * lax.rsqrt(jnp.mean(x * x, axis=-1, keepdims=True) + eps) * g


def _params(n_axes):
    return pltpu.CompilerParams(
        dimension_semantics=("arbitrary",) * n_axes, vmem_limit_bytes=VMEM_LIMIT_BYTES)


def _resident(shape):
    nd = len(shape)
    return pl.BlockSpec(shape, lambda *_: (0,) * nd, pipeline_mode=pl.Buffered(1))


def _proj_kernel(x_ref, g_ref, w_ref, b_ref, o_ref, *, n_slabs, dot_slabs, n_scaled, scale):
    h = _rms(x_ref[...], g_ref[...], NORM_EPS).astype(BF16)
    for c0 in range(0, n_slabs, dot_slabs):
        nc = min(dot_slabs, n_slabs - c0)
        cols = slice(c0 * LANES, (c0 + nc) * LANES)
        y = jnp.dot(h, w_ref[:, cols], preferred_element_type=F32) + b_ref[:, cols]
        for c in range(nc):
            slab = y[:, c * LANES:(c + 1) * LANES]
            if c0 + c < n_scaled:
                slab = slab * scale
            o_ref[c0 + c] = slab.astype(BF16)


def _proj(x, g, w, b, n_scaled=0, scale=1.0):
    T, D = x.shape
    N = w.shape[1]
    n_slabs = N // LANES
    tm = ROW_TILE
    return pl.pallas_call(
        functools.partial(_proj_kernel, n_slabs=n_slabs, dot_slabs=4, n_scaled=n_scaled, scale=scale),
        grid=(T // tm,),
        in_specs=[
            pl.BlockSpec((tm, D), lambda i: (i, 0)),
            _resident((1, D)),
            _resident((D, N)),
            _resident((1, N)),
        ],
        out_specs=pl.BlockSpec((n_slabs, tm, LANES), lambda i: (0, i, 0)),
        out_shape=jax.ShapeDtypeStruct((n_slabs, T, LANES), BF16),
        compiler_params=_params(1),
        name="norm_proj",
    )(x, g.reshape(1, D), w, b.reshape(1, N))


def _attn_a_kernel(q_ref, k_ref, v_ref, kpos_ref, lq_ref, sg_ref, o_ref,
                   kk_scr, vt_scr, qq_scr, s_scr, m_scr, acc_scr, *, lambda_init):
    qi = pl.program_id(2)
    tq = q_ref.shape[0]
    tk = ATTN_KEY_TILE
    halves = tq // tk
    n_chunks = 2 * halves
    assert n_chunks % 2 == 0

    @pl.when(qi == 0)
    def _():
        kk_scr[:, :LANES] = k_ref[...]
        kk_scr[:, LANES:] = kpos_ref[...]
        for t in range(vt_scr.shape[0]):
            vt_scr[t, :LANES, :] = v_ref[t * tk:(t + 1) * tk, :].T
            vt_scr[t, LANES:, :] = jnp.ones((SUM_ROWS, tk), BF16)

    q = q_ref[...]
    lane = lax.broadcasted_iota(jnp.int32, (tq, LANES), 1)
    zero = jnp.zeros_like(q)
    ones_cols = jnp.where(lane < POS_PARTS, 1.0, 0.0).astype(BF16)
    qq_scr[:tq, :LANES] = jnp.where(lane < HEAD_DIM, q, zero)
    qq_scr[tq:, :LANES] = jnp.where(lane >= HEAD_DIM, q, zero)
    qq_scr[:tq, LANES:] = ones_cols
    qq_scr[tq:, LANES:] = ones_cols

    m_scr[...] = jnp.full(m_scr.shape, -jnp.inf, F32)
    acc_scr[...] = jnp.zeros(acc_scr.shape, F32)

    def scores(ki, c, slot):
        start = pl.multiple_of(ki * tk, tk)
        s_scr[slot] = lax.dot_general(
            kk_scr[pl.ds(start, tk), :], qq_scr[c * tk:(c + 1) * tk, :], (((1,), (1,)), ((), ())),
            preferred_element_type=F32)

    def softmax_pv(ki, c, slot, triangular):
        cols = slice(c * tk, (c + 1) * tk)
        s = s_scr[slot]
        if triangular:
            key = lax.broadcasted_iota(jnp.int32, (tk, tk), 0)
            qry = lax.broadcasted_iota(jnp.int32, (tk, tk), 1)
            s = jnp.where(key <= qry, s, -jnp.inf)
        m_prev = m_scr[:, cols]
        m_new = jnp.maximum(m_prev, jnp.max(s, axis=0, keepdims=True))
        alpha = jnp.exp2(m_prev - m_new)
        p = jnp.exp2(s - m_new)
        acc_scr[:, cols] = alpha * acc_scr[:, cols] + jnp.dot(
            vt_scr[ki], p.astype(BF16), preferred_element_type=F32)
        m_scr[:, cols] = m_new

    scores(0, 0, 0)

    def body(ki, carry):
        for c in range(n_chunks):
            if c + 1 < n_chunks:
                scores(ki, c + 1, (c + 1) % 2)
            else:
                scores(ki + 1, 0, 0)
            softmax_pv(ki, c, c % 2, False)
        return carry

    first_diag = halves * qi
    lax.fori_loop(0, first_diag, body, 0)

    items = [(d, c) for d in range(halves) for c in range(n_chunks) if c % halves >= d]
    for n, (d, c) in enumerate(items):
        if n + 1 < len(items):
            d_next, c_next = items[n + 1]
            scores(first_diag + d_next, c_next, (n + 1) % 2)
        softmax_pv(first_diag + d, c, n % 2, c % halves == d)

    lq = lq_ref[...]
    lam = (jnp.exp(jnp.sum(lq[0:1] * lq[1:2], axis=-1, keepdims=True))
           - jnp.exp(jnp.sum(lq[2:3] * lq[3:4], axis=-1, keepdims=True)) + lambda_init)
    l = acc_scr[LANES:LANES + 1, :]
    o = (acc_scr[:LANES, :tq] / l[:, :tq] - lam * (acc_scr[:LANES, tq:] / l[:, tq:]))
    o = o * lax.rsqrt(jnp.mean(o * o, axis=0, keepdims=True) + SUBLN_EPS) * sg_ref[...]
    o_ref[...] = (o * (1.0 - lambda_init)).T.astype(BF16)


def _alibi_key_columns(S):
    j = np.arange(S, dtype=np.float64)
    cols = np.zeros((DIFF_HEADS, S, LANES), np.float32)
    for h, slope in enumerate(_alibi_slopes(DIFF_HEADS)):
        rest = slope * LOG2E * j
        for part in range(POS_PARTS):
            piece = rest.astype(ml_dtypes.bfloat16).astype(np.float64)
            cols[h, :, part] = piece
            rest = rest - piece
    return jnp.asarray(cols, BF16)


def _attn_a(qkv3, lam_qk, subln_g, lambda_init, B, S):
    T = B * S
    H = DIFF_HEADS
    tq = ATTN_QUERY_TILE
    tk = ATTN_KEY_TILE
    nq = S // tq
    kernel = functools.partial(_attn_a_kernel, lambda_init=lambda_init)
    return pl.pallas_call(
        kernel,
        grid=(B, H, nq),
        in_specs=[
            pl.BlockSpec((None, tq, LANES), lambda b, h, qi: (h, b * nq + qi, 0)),
            pl.BlockSpec((None, S, LANES), lambda b, h, qi: (H + h, b, 0)),
            pl.BlockSpec((None, S, LANES), lambda b, h, qi: (2 * H + h, b, 0)),
            pl.BlockSpec((None, S, LANES), lambda b, h, qi: (h, 0, 0)),
            _resident((4, HEAD_DIM)),
            _resident((LANES, 1)),
        ],
        out_specs=pl.BlockSpec((tq, LANES), lambda b, h, qi: (b * nq + qi, h)),
        out_shape=jax.ShapeDtypeStruct((T, D_MODEL), BF16),
        scratch_shapes=[
            pltpu.VMEM((S, 2 * LANES), BF16),
            pltpu.VMEM((S // tk, LANES + SUM_ROWS, tk), BF16),
            pltpu.VMEM((2 * tq, 2 * LANES), BF16),
            pltpu.VMEM((2, tk, tk), F32),
            pltpu.VMEM((1, 2 * tq), F32),
            pltpu.VMEM((LANES + SUM_ROWS, 2 * tq), F32),
        ],
        compiler_params=_params(3),
        name="diff_attn",
    )(qkv3, qkv3, qkv3, _alibi_key_columns(S), lam_qk, subln_g.reshape(LANES, 1))


def _attn_b_kernel(sinks_ref, q_ref, k_ref, v_ref, bias_ref, o_ref):
    lane = lax.broadcasted_iota(jnp.int32, (WINDOW, LANES), 1)
    pairs_per_kv = SWA_GROUP // 2
    group_lanes = SWA_GROUP * WINDOW
    ones_rows = jnp.ones((SUM_ROWS, 2 * WINDOW), BF16)

    def window_start(qb):
        nb = pl.program_id(1) * SWA_BLOCKS_PER_STEP + qb
        return nb, pl.multiple_of(jnp.maximum(nb - 1, 0) * WINDOW, WINDOW)

    def scores(qb, kh):
        _, start = window_start(qb)
        kwin = k_ref[kh, pl.ds(start, 2 * WINDOW), :]
        rows = []
        for cpair in range(pairs_per_kv):
            qc = q_ref[kh * pairs_per_kv + cpair, qb * WINDOW:(qb + 1) * WINDOW, :]
            zero = jnp.zeros_like(qc)
            rows.append(jnp.where(lane < HEAD_DIM, qc, zero))
            rows.append(jnp.where(lane >= HEAD_DIM, qc, zero))
        return lax.dot_general(kwin, jnp.concatenate(rows, axis=0), (((1,), (1,)), ((), ())),
                               preferred_element_type=F32)

    def softmax_pv(qb, kh, s):
        nb, start = window_start(qb)
        bias = bias_ref.at[jnp.minimum(nb, 1) if qb == 0 else 1]
        s = s + bias[:, kh * group_lanes:(kh + 1) * group_lanes]
        sink = jnp.concatenate(
            [jnp.full((1, WINDOW), sinks_ref[kh * SWA_GROUP + g] * LOG2E, F32) for g in range(SWA_GROUP)],
            axis=1)
        m = jnp.maximum(jnp.max(s, axis=0, keepdims=True), sink)
        p = jnp.exp2(s - m).astype(BF16)
        vt = v_ref[pl.ds(start, 2 * WINDOW), :].T[kh * HEAD_DIM:(kh + 1) * HEAD_DIM]
        o = jnp.dot(jnp.concatenate([vt, ones_rows], axis=0), p,
                    preferred_element_type=F32)
        denom = o[HEAD_DIM:HEAD_DIM + 1] + jnp.exp2(sink - m)
        o = o[:HEAD_DIM] * (1.0 / denom)
        for cpair in range(pairs_per_kv):
            pair = jnp.concatenate([o[:, (2 * cpair) * WINDOW:(2 * cpair + 1) * WINDOW],
                                    o[:, (2 * cpair + 1) * WINDOW:(2 * cpair + 2) * WINDOW]], axis=0)
            col = (kh * pairs_per_kv + cpair) * LANES
            o_ref[qb * WINDOW:(qb + 1) * WINDOW, col:col + LANES] = pair.T.astype(BF16)

    items = [(qb, kh) for qb in range(SWA_BLOCKS_PER_STEP) for kh in range(SWA_KV_HEADS)]
    s_next = scores(*items[0])
    for n, item in enumerate(items):
        s_cur = s_next
        if n + 1 < len(items):
            s_next = scores(*items[n + 1])
        softmax_pv(*item, s_cur)


def _swa_bias():
    key = np.arange(2 * WINDOW)[:, None]
    qry = np.arange(WINDOW)[None, :]
    bias = np.full((2, 2 * WINDOW, SWA_Q_HEADS * WINDOW), -np.inf, np.float32)
    for case, offset in enumerate((0, WINDOW)):
        dist = offset + qry - key
        valid = (dist >= 0) & (dist < WINDOW)
        for h, slope in enumerate(_alibi_slopes(SWA_Q_HEADS)):
            bias[case, :, h * WINDOW:(h + 1) * WINDOW] = np.where(valid, -slope * LOG2E * dist, -np.inf)
    return jnp.asarray(bias)


def _attn_b(q3, k3, v3, sinks, B, S):
    T = B * S
    rows = SWA_BLOCKS_PER_STEP * WINDOW
    nstep = S // rows
    n_pairs = SWA_Q_HEADS // 2
    return pl.pallas_call(
        _attn_b_kernel,
        grid_spec=pltpu.PrefetchScalarGridSpec(
            num_scalar_prefetch=1,
            grid=(B, nstep),
            in_specs=[
                pl.BlockSpec((n_pairs, rows, LANES), lambda b, n, *_: (0, b * nstep + n, 0)),
                pl.BlockSpec((SWA_KV_HEADS, S, LANES), lambda b, n, *_: (0, b, 0)),
                pl.BlockSpec((None, S, LANES), lambda b, n, *_: (SWA_KV_HEADS, b, 0)),
                _resident((2, 2 * WINDOW, SWA_Q_HEADS * WINDOW)),
            ],
            out_specs=pl.BlockSpec((rows, D_MODEL), lambda b, n, *_: (b * nstep + n, 0)),
        ),
        out_shape=jax.ShapeDtypeStruct((T, D_MODEL), BF16),
        compiler_params=_params(2),
        name="swa_attn",
    )(sinks, q3, k3, v3, _swa_bias())


def _post_kernel(o_ref, w_ref, x_ref, g_ref, out_ref):
    mix = jnp.dot(o_ref[...], w_ref[...], preferred_element_type=F32)
    out_ref[...] = x_ref[...] + _rms(mix, g_ref[...], NORM_EPS)


def _post(o, w, x, g):
    T, D = x.shape
    tm = ROW_TILE
    return pl.pallas_call(
        _post_kernel,
        grid=(T // tm,),
        in_specs=[
            pl.BlockSpec((tm, D), lambda i: (i, 0)),
            _resident((D, D)),
            pl.BlockSpec((tm, D), lambda i: (i, 0)),
            _resident((1, D)),
        ],
        out_specs=pl.BlockSpec((tm, D), lambda i: (i, 0)),
        out_shape=jax.ShapeDtypeStruct((T, D), F32),
        input_output_aliases={2: 0},
        compiler_params=_params(1),
        name="out_proj_residual",
    )(o, w, x, g.reshape(1, D))


def _gelu_tanh(x):
    return 0.5 * x * (1.0 + jnp.tanh(math.sqrt(2.0 / math.pi) * (x + 0.044715 * (x * x * x))))


def _ffn_kernel(x_ref, g2_ref, g3_ref, wgu_ref, wd_ref, cw_ref, cb_ref, out_ref,
                h_scr, a_scr, carry_scr, gate0_scr, gate1_scr, up0_scr, up1_scr, *, tiles_per_seq):
    i = pl.program_id(0)
    tm = x_ref.shape[0]
    rb = FF_ROW_BLOCK
    n_blocks = tm // rb
    tf = FF_CHUNK
    slots = ((gate0_scr, up0_scr), (gate1_scr, up1_scr))

    @pl.when(i == 0)
    def _():
        carry_scr[...] = jnp.zeros(carry_scr.shape, F32)

    h_scr[...] = _rms(x_ref[...], g2_ref[...], NORM_EPS).astype(BF16)
    seq_start = (i % tiles_per_seq) == 0

    def up_proj(r):
        gate_scr, up_scr = slots[r % 2]
        if r == 0:
            gate_scr[0:8, :] = jnp.where(seq_start, 0.0, carry_scr[...])
        else:
            gate_scr[0:8, :] = slots[(r - 1) % 2][0][rb:rb + 8, :]
        h = h_scr[r * rb:(r + 1) * rb, :]
        for c in range(0, D_FF, FF_DOT_COLS):
            w = min(FF_DOT_COLS, D_FF - c)
            gate_scr[8:8 + rb, c:c + w] = jnp.dot(h, wgu_ref[:, c:c + w], preferred_element_type=F32)
            up_scr[:, c:c + w] = jnp.dot(h, wgu_ref[:, D_FF + c:D_FF + c + w],
                                         preferred_element_type=F32)

    def act_down(r):
        gate_scr, up_scr = slots[r % 2]
        rows = slice(r * rb, (r + 1) * rb)
        for c in range(0, D_FF, tf):
            cols = slice(c, c + tf)
            conv = (cw_ref[2:3, cols] * gate_scr[8:8 + rb, cols] + cb_ref[:, cols]
                    + cw_ref[1:2, cols] * gate_scr[7:7 + rb, cols]
                    + cw_ref[0:1, cols] * gate_scr[6:6 + rb, cols])
            a_scr[:, cols] = (_gelu_tanh(conv) * up_scr[:, cols]).astype(BF16)
        f = jnp.dot(a_scr[...], wd_ref[...], preferred_element_type=F32)
        out_ref[rows, :] = x_ref[rows, :] + _rms(f, g3_ref[...], NORM_EPS)
        if r == n_blocks - 1:
            carry_scr[...] = gate_scr[rb:rb + 8, :]

    up_proj(0)
    for r in range(n_blocks):
        if r + 1 < n_blocks:
            up_proj(r + 1)
        act_down(r)


def _ffn(x, g2, g3, w_up, conv_w, conv_b, w_down, S):
    T, D = x.shape
    tm = FF_ROW_TILE
    rb = FF_ROW_BLOCK
    kernel = functools.partial(_ffn_kernel, tiles_per_seq=S // tm)
    return pl.pallas_call(
        kernel,
        grid=(T // tm,),
        in_specs=[
            pl.BlockSpec((tm, D), lambda i: (i, 0)),
            _resident((1, D)),
            _resident((1, D)),
            _resident((D, 2 * D_FF)),
            _resident((D_FF, D)),
            _resident((CONV_WIDTH, D_FF)),
            _resident((1, D_FF)),
        ],
        out_specs=pl.BlockSpec((tm, D), lambda i: (i, 0)),
        out_shape=jax.ShapeDtypeStruct((T, D), F32),
        scratch_shapes=[
            pltpu.VMEM((tm, D), BF16),
            pltpu.VMEM((rb, D_FF), BF16),
            pltpu.VMEM((8, D_FF), F32),
            pltpu.VMEM((rb + 8, D_FF), F32),
            pltpu.VMEM((rb + 8, D_FF), F32),
            pltpu.VMEM((rb, D_FF), F32),
            pltpu.VMEM((rb, D_FF), F32),
        ],
        input_output_aliases={0: 0},
        compiler_params=_params(1),
        name="conv_ffn",
    )(x, g2.reshape(1, D), g3.reshape(1, D), w_up.astype(BF16), w_down.astype(BF16), conv_w,
      conv_b.reshape(1, D_FF))


def _dup_heads(w, n_heads):
    lead = w.shape[:-1]
    w = w.reshape(lead + (n_heads, 1, HEAD_DIM))
    return jnp.broadcast_to(w, lead + (n_heads, 2, HEAD_DIM)).reshape(lead + (2 * n_heads * HEAD_DIM,))


def kernel(x, norm_gains, w_qkv_a, lambda_qk_a, subln_a, w_o_a, kv_norm, w_kv_b, b_kv_b, w_q_b, b_q_b,
           sinks_b, w_o_b, w_up, conv_w, conv_b, w_down):
    B, S, D = x.shape
    T = B * S
    assert D == D_MODEL and S % ATTN_QUERY_TILE == 0 and S % ROW_TILE == 0 and S % FF_ROW_TILE == 0 and S >= 2 * WINDOW
    xt = x.reshape(T, D)
    kv3 = None
    for layer in range(DEPTH):
        g = norm_gains[layer]
        if layer < N_A_LAYERS:
            lambda_init = 0.8 - 0.6 * math.exp(-0.3 * layer)
            qkv3 = _proj(xt, g[0], w_qkv_a[layer].astype(BF16), jnp.zeros((3 * D,), F32),
                         n_scaled=DIFF_HEADS, scale=QK_SCALE * LOG2E)
            o = _attn_a(qkv3, lambda_qk_a[layer], subln_a[layer], lambda_init, B, S)
            xt = _post(o, w_o_a[layer].astype(BF16), xt, g[1])
        else:
            j = layer - N_A_LAYERS
            if kv3 is None:
                kvw = SWA_KV_HEADS * HEAD_DIM
                w_kv = jnp.concatenate([_dup_heads(w_kv_b[:, :kvw], SWA_KV_HEADS), w_kv_b[:, kvw:]], axis=1)
                b_kv = jnp.concatenate([_dup_heads(b_kv_b[:kvw], SWA_KV_HEADS), b_kv_b[kvw:]])
                kv3 = _proj(xt, kv_norm, w_kv.astype(BF16), b_kv)
            q3 = _proj(xt, g[0], w_q_b[j].astype(BF16), b_q_b[j],
                       n_scaled=SWA_Q_HEADS // 2, scale=QK_SCALE * LOG2E)
            o = _attn_b(q3, kv3, kv3, sinks_b[j], B, S)
            xt = _post(o, w_o_b[j].astype(BF16), xt, g[1])
        xt = _ffn(xt, g[2], g[3], w_up[layer], conv_w[layer], conv_b[layer], w_down[layer], S)
    return xt.reshape(B, S, D)
```

```python
import functools
import math

import ml_dtypes
import numpy as np
import jax
import jax.numpy as jnp
from jax import lax
from jax.experimental import pallas as pl
from jax.experimental.pallas import tpu as pltpu

D_MODEL = 1024
HEAD_DIM = 64
DEPTH = 4
N_A_LAYERS = DEPTH // 2
DIFF_HEADS = D_MODEL // (2 * HEAD_DIM)
SWA_Q_HEADS = D_MODEL // HEAD_DIM
SWA_KV_HEADS = 2
SWA_GROUP = SWA_Q_HEADS // SWA_KV_HEADS
WINDOW = 128
D_FF = 2816
CONV_WIDTH = 3
NORM_EPS = 1e-6
SUBLN_EPS = 1e-5

LANES = 128
QK_SCALE = HEAD_DIM ** -0.5
VMEM_LIMIT_BYTES = 56 * 1024 * 1024
ROW_TILE = 512
ATTN_QUERY_TILE = 2048
ATTN_KEY_TILE = 512
SUM_ROWS = 16
SWA_BLOCKS_PER_STEP = 8
FF_CHUNK = 256
FF_DOT_COLS = 512
FF_ROW_TILE = 512
FF_ROW_BLOCK = 256
POS_PARTS = 3
LOG2E = math.log2(math.e)

BF16 = jnp.bfloat16
F32 = jnp.float32


def _alibi_slopes(n):
    return [2.0 ** (-8.0 * (i + 1) / n) for i in range(n)]


def _rms(x, g, eps):
    return x * lax.rsqrt(jnp.mean(x * x, axis=-1, keepdims=True) + eps) * g


def _params(n_axes):
    return pltpu.CompilerParams(
        dimension_semantics=("arbitrary",) * n_axes, vmem_limit_bytes=VMEM_LIMIT_BYTES)


def _resident(shape):
    nd = len(shape)
    return pl.BlockSpec(shape, lambda *_: (0,) * nd, pipeline_mode=pl.Buffered(1))


def _proj_kernel(x_ref, g_ref, w_ref, b_ref, o_ref, *, n_slabs, dot_slabs, n_scaled, scale):
    h = _rms(x_ref[...], g_ref[...], NORM_EPS).astype(BF16)
    for c0 in range(0, n_slabs, dot_slabs):
        nc = min(dot_slabs, n_slabs - c0)
        cols = slice(c0 * LANES, (c0 + nc) * LANES)
        y = jnp.dot(h, w_ref[:, cols], preferred_element_type=F32) + b_ref[:, cols]
        for c in range(nc):
            slab = y[:, c * LANES:(c + 1) * LANES]
            if c0 + c < n_scaled:
                slab = slab * scale
            o_ref[c0 + c] = slab.astype(BF16)


def _proj(x, g, w, b, n_scaled=0, scale=1.0):
    T, D = x.shape
    N = w.shape[1]
    n_slabs = N // LANES
    tm = ROW_TILE
    return pl.pallas_call(
        functools.partial(_proj_kernel, n_slabs=n_slabs, dot_slabs=4, n_scaled=n_scaled, scale=scale),
        grid=(T // tm,),
        in_specs=[
            pl.BlockSpec((tm, D), lambda i: (i, 0)),
            _resident((1, D)),
            _resident((D, N)),
            _resident((1, N)),
        ],
        out_specs=pl.BlockSpec((n_slabs, tm, LANES), lambda i: (0, i, 0)),
        out_shape=jax.ShapeDtypeStruct((n_slabs, T, LANES), BF16),
        compiler_params=_params(1),
        name="norm_proj",
    )(x, g.reshape(1, D), w, b.reshape(1, N))


def _attn_a_kernel(q_ref, k_ref, v_ref, kpos_ref, lq_ref, sg_ref, o_ref,
                   kk_scr, vt_scr, qq_scr, s_scr, m_scr, acc_scr, *, lambda_init):
    qi = pl.program_id(2)
    tq = q_ref.shape[0]
    tk = ATTN_KEY_TILE
    halves = tq // tk
    n_chunks = 2 * halves
    assert n_chunks % 2 == 0

    @pl.when(qi == 0)
    def _():
        kk_scr[:, :LANES] = k_ref[...]
        kk_scr[:, LANES:] = kpos_ref[...]
        for t in range(vt_scr.shape[0]):
            vt_scr[t, :LANES, :] = v_ref[t * tk:(t + 1) * tk, :].T
            vt_scr[t, LANES:, :] = jnp.ones((SUM_ROWS, tk), BF16)

    q = q_ref[...]
    lane = lax.broadcasted_iota(jnp.int32, (tq, LANES), 1)
    zero = jnp.zeros_like(q)
    ones_cols = jnp.where(lane < POS_PARTS, 1.0, 0.0).astype(BF16)
    qq_scr[:tq, :LANES] = jnp.where(lane < HEAD_DIM, q, zero)
    qq_scr[tq:, :LANES] = jnp.where(lane >= HEAD_DIM, q, zero)
    qq_scr[:tq, LANES:] = ones_cols
    qq_scr[tq:, LANES:] = ones_cols

    m_scr[...] = jnp.full(m_scr.shape, -jnp.inf, F32)
    acc_scr[...] = jnp.zeros(acc_scr.shape, F32)

    def scores(ki, c, slot):
        start = pl.multiple_of(ki * tk, tk)
        s_scr[slot] = lax.dot_general(
            kk_scr[pl.ds(start, tk), :], qq_scr[c * tk:(c + 1) * tk, :], (((1,), (1,)), ((), ())),
            preferred_element_type=F32)

    def softmax_pv(ki, c, slot, triangular):
        cols = slice(c * tk, (c + 1) * tk)
        s = s_scr[slot]
        if triangular:
            key = lax.broadcasted_iota(jnp.int32, (tk, tk), 0)
            qry = lax.broadcasted_iota(jnp.int32, (tk, tk), 1)
            s = jnp.where(key <= qry, s, -jnp.inf)
        m_prev = m_scr[:, cols]
        m_new = jnp.maximum(m_prev, jnp.max(s, axis=0, keepdims=True))
        alpha = jnp.exp2(m_prev - m_new)
        p = jnp.exp2(s - m_new)
        acc_scr[:, cols] = alpha * acc_scr[:, cols] + jnp.dot(
            vt_scr[ki], p.astype(BF16), preferred_element_type=F32)
        m_scr[:, cols] = m_new

    scores(0, 0, 0)

    def body(ki, carry):
        for c in range(n_chunks):
            if c + 1 < n_chunks:
                scores(ki, c + 1, (c + 1) % 2)
            else:
                scores(ki + 1, 0, 0)
            softmax_pv(ki, c, c % 2, False)
        return carry

    first_diag = halves * qi
    lax.fori_loop(0, first_diag, body, 0)

    items = [(d, c) for d in range(halves) for c in range(n_chunks) if c % halves >= d]
    for n, (d, c) in enumerate(items):
        if n + 1 < len(items):
            d_next, c_next = items[n + 1]
            scores(first_diag + d_next, c_next, (n + 1) % 2)
        softmax_pv(first_diag + d, c, n % 2, c % halves == d)

    lq = lq_ref[...]
    lam = (jnp.exp(jnp.sum(lq[0:1] * lq[1:2], axis=-1, keepdims=True))
           - jnp.exp(jnp.sum(lq[2:3] * lq[3:4], axis=-1, keepdims=True)) + lambda_init)
    l = acc_scr[LANES:LANES + 1, :]
    o = (acc_scr[:LANES, :tq] / l[:, :tq] - lam * (acc_scr[:LANES, tq:] / l[:, tq:]))
    o = o * lax.rsqrt(jnp.mean(o * o, axis=0, keepdims=True) + SUBLN_EPS) * sg_ref[...]
    o_ref[...] = (o * (1.0 - lambda_init)).T.astype(BF16)


def _alibi_key_columns(S):
    j = np.arange(S, dtype=np.float64)
    cols = np.zeros((DIFF_HEADS, S, LANES), np.float32)
    for h, slope in enumerate(_alibi_slopes(DIFF_HEADS)):
        rest = slope * LOG2E * j
        for part in range(POS_PARTS):
            piece = rest.astype(ml_dtypes.bfloat16).astype(np.float64)
            cols[h, :, part] = piece
            rest = rest - piece
    return jnp.asarray(cols, BF16)


def _attn_a(qkv3, lam_qk, subln_g, lambda_init, B, S):
    T = B * S
    H = DIFF_HEADS
    tq = ATTN_QUERY_TILE
    tk = ATTN_KEY_TILE
    nq = S // tq
    kernel = functools.partial(_attn_a_kernel, lambda_init=lambda_init)
    return pl.pallas_call(
        kernel,
        grid=(B, H, nq),
        in_specs=[
            pl.BlockSpec((None, tq, LANES), lambda b, h, qi: (h, b * nq + qi, 0)),
            pl.BlockSpec((None, S, LANES), lambda b, h, qi: (H + h, b, 0)),
            pl.BlockSpec((None, S, LANES), lambda b, h, qi: (2 * H + h, b, 0)),
            pl.BlockSpec((None, S, LANES), lambda b, h, qi: (h, 0, 0)),
            _resident((4, HEAD_DIM)),
            _resident((LANES, 1)),
        ],
        out_specs=pl.BlockSpec((tq, LANES), lambda b, h, qi: (b * nq + qi, h)),
        out_shape=jax.ShapeDtypeStruct((T, D_MODEL), BF16),
        scratch_shapes=[
            pltpu.VMEM((S, 2 * LANES), BF16),
            pltpu.VMEM((S // tk, LANES + SUM_ROWS, tk), BF16),
            pltpu.VMEM((2 * tq, 2 * LANES), BF16),
            pltpu.VMEM((2, tk, tk), F32),
            pltpu.VMEM((1, 2 * tq), F32),
            pltpu.VMEM((LANES + SUM_ROWS, 2 * tq), F32),
        ],
        compiler_params=_params(3),
        name="diff_attn",
    )(qkv3, qkv3, qkv3, _alibi_key_columns(S), lam_qk, subln_g.reshape(LANES, 1))


def _attn_b_kernel(sinks_ref, q_ref, k_ref, v_ref, bias_ref, o_ref):
    lane = lax.broadcasted_iota(jnp.int32, (WINDOW, LANES), 1)
    pairs_per_kv = SWA_GROUP // 2
    group_lanes = SWA_GROUP * WINDOW
    ones_rows = jnp.ones((SUM_ROWS, 2 * WINDOW), BF16)

    def window_start(qb):
        nb = pl.program_id(1) * SWA_BLOCKS_PER_STEP + qb
        return nb, pl.multiple_of(jnp.maximum(nb - 1, 0) * WINDOW, WINDOW)

    def scores(qb, kh):
        _, start = window_start(qb)
        kwin = k_ref[kh, pl.ds(start, 2 * WINDOW), :]
        rows = []
        for cpair in range(pairs_per_kv):
            qc = q_ref[kh * pairs_per_kv + cpair, qb * WINDOW:(qb + 1) * WINDOW, :]
            zero = jnp.zeros_like(qc)
            rows.append(jnp.where(lane < HEAD_DIM, qc, zero))
            rows.append(jnp.where(lane >= HEAD_DIM, qc, zero))
        return lax.dot_general(kwin, jnp.concatenate(rows, axis=0), (((1,), (1,)), ((), ())),
                               preferred_element_type=F32)

    def softmax_pv(qb, kh, s):
        nb, start = window_start(qb)
        bias = bias_ref.at[jnp.minimum(nb, 1) if qb == 0 else 1]
        s = s + bias[:, kh * group_lanes:(kh + 1) * group_lanes]
        sink = jnp.concatenate(
            [jnp.full((1, WINDOW), sinks_ref[kh * SWA_GROUP + g] * LOG2E, F32) for g in range(SWA_GROUP)],
            axis=1)
        m = jnp.maximum(jnp.max(s, axis=0, keepdims=True), sink)
        p = jnp.exp2(s - m).astype(BF16)
        vt = v_ref[pl.ds(start, 2 * WINDOW), :].T[kh * HEAD_DIM:(kh + 1) * HEAD_DIM]
        o = jnp.dot(jnp.concatenate([vt, ones_rows], axis=0), p,
                    preferred_element_type=F32)
        denom = o[HEAD_DIM:HEAD_DIM + 1] + jnp.exp2(sink - m)
        o = o[:HEAD_DIM] * (1.0 / denom)
        for cpair in range(pairs_per_kv):
            pair = jnp.concatenate([o[:, (2 * cpair) * WINDOW:(2 * cpair + 1) * WINDOW],
                                    o[:, (2 * cpair + 1) * WINDOW:(2 * cpair + 2) * WINDOW]], axis=0)
            col = (kh * pairs_per_kv + cpair) * LANES
            o_ref[qb * WINDOW:(qb + 1) * WINDOW, col:col + LANES] = pair.T.astype(BF16)

    items = [(qb, kh) for qb in range(SWA_BLOCKS_PER_STEP) for kh in range(SWA_KV_HEADS)]
    s_next = scores(*items[0])
    for n, item in enumerate(items):
        s_cur = s_next
        if n + 1 < len(items):
            s_next = scores(*items[n + 1])
        softmax_pv(*item, s_cur)


def _swa_bias():
    key = np.arange(2 * WINDOW)[:, None]
    qry = np.arange(WINDOW)[None, :]
    bias = np.full((2, 2 * WINDOW, SWA_Q_HEADS * WINDOW), -np.inf, np.float32)
    for case, offset in enumerate((0, WINDOW)):
        dist = offset + qry - key
        valid = (dist >= 0) & (dist < WINDOW)
        for h, slope in enumerate(_alibi_slopes(SWA_Q_HEADS)):
            bias[case, :, h * WINDOW:(h + 1) * WINDOW] = np.where(valid, -slope * LOG2E * dist, -np.inf)
    return jnp.asarray(bias)


def _attn_b(q3, k3, v3, sinks, B, S):
    T = B * S
    rows = SWA_BLOCKS_PER_STEP * WINDOW
    nstep = S // rows
    n_pairs = SWA_Q_HEADS // 2
    return pl.pallas_call(
        _attn_b_kernel,
        grid_spec=pltpu.PrefetchScalarGridSpec(
            num_scalar_prefetch=1,
            grid=(B, nstep),
            in_specs=[
                pl.BlockSpec((n_pairs, rows, LANES), lambda b, n, *_: (0, b * nstep + n, 0)),
                pl.BlockSpec((SWA_KV_HEADS, S, LANES), lambda b, n, *_: (0, b, 0)),
                pl.BlockSpec((None, S, LANES), lambda b, n, *_: (SWA_KV_HEADS, b, 0)),
                _resident((2, 2 * WINDOW, SWA_Q_HEADS * WINDOW)),
            ],
            out_specs=pl.BlockSpec((rows, D_MODEL), lambda b, n, *_: (b * nstep + n, 0)),
        ),
        out_shape=jax.ShapeDtypeStruct((T, D_MODEL), BF16),
        compiler_params=_params(2),
        name="swa_attn",
    )(sinks, q3, k3, v3, _swa_bias())


def _gelu_tanh(x):
    return 0.5 * x * (1.0 + jnp.tanh(math.sqrt(2.0 / math.pi) * (x + 0.044715 * (x * x * x))))


def _ffn_kernel(o_ref, x_ref, g1_ref, g2_ref, g3_ref, wo_ref, wgu_ref, wd_ref, cw_ref, cb_ref, out_ref,
                x1_scr, a_scr, carry_scr, gate0_scr, gate1_scr, up0_scr, up1_scr, *, tiles_per_seq):
    i = pl.program_id(0)
    tm = x_ref.shape[0]
    rb = FF_ROW_BLOCK
    n_blocks = tm // rb
    tf = FF_CHUNK
    slots = ((gate0_scr, up0_scr), (gate1_scr, up1_scr))

    @pl.when(i == 0)
    def _():
        carry_scr[...] = jnp.zeros(carry_scr.shape, F32)

    seq_start = (i % tiles_per_seq) == 0

    def up_proj(r):
        gate_scr, up_scr = slots[r % 2]
        rows = slice(r * rb, (r + 1) * rb)
        if r == 0:
            gate_scr[0:8, :] = jnp.where(seq_start, 0.0, carry_scr[...])
        else:
            gate_scr[0:8, :] = slots[(r - 1) % 2][0][rb:rb + 8, :]
        mix = jnp.dot(o_ref[rows, :], wo_ref[...], preferred_element_type=F32)
        x1 = x_ref[rows, :] + _rms(mix, g1_ref[...], NORM_EPS)
        x1_scr[rows, :] = x1
        h = _rms(x1, g2_ref[...], NORM_EPS).astype(BF16)
        for c in range(0, D_FF, FF_DOT_COLS):
            w = min(FF_DOT_COLS, D_FF - c)
            gate_scr[8:8 + rb, c:c + w] = jnp.dot(h, wgu_ref[:, c:c + w], preferred_element_type=F32)
            up_scr[:, c:c + w] = jnp.dot(h, wgu_ref[:, D_FF + c:D_FF + c + w],
                                         preferred_element_type=F32)

    def act_down(r):
        gate_scr, up_scr = slots[r % 2]
        rows = slice(r * rb, (r + 1) * rb)
        for c in range(0, D_FF, tf):
            cols = slice(c, c + tf)
            conv = (cw_ref[2:3, cols] * gate_scr[8:8 + rb, cols] + cb_ref[:, cols]
                    + cw_ref[1:2, cols] * gate_scr[7:7 + rb, cols]
                    + cw_ref[0:1, cols] * gate_scr[6:6 + rb, cols])
            a_scr[:, cols] = (_gelu_tanh(conv) * up_scr[:, cols]).astype(BF16)
        f = jnp.dot(a_scr[...], wd_ref[...], preferred_element_type=F32)
        out_ref[rows, :] = x1_scr[rows, :] + _rms(f, g3_ref[...], NORM_EPS)
        if r == n_blocks - 1:
            carry_scr[...] = gate_scr[rb:rb + 8, :]

    up_proj(0)
    for r in range(n_blocks):
        if r + 1 < n_blocks:
            up_proj(r + 1)
        act_down(r)


def _post_ffn(o, x, g1, g2, g3, w_o, w_up, conv_w, conv_b, w_down, S, in_place):
    T, D = x.shape
    tm = FF_ROW_TILE
    rb = FF_ROW_BLOCK
    kernel = functools.partial(_ffn_kernel, tiles_per_seq=S // tm)
    return pl.pallas_call(
        kernel,
        grid=(T // tm,),
        in_specs=[
            pl.BlockSpec((tm, D), lambda i: (i, 0)),
            pl.BlockSpec((tm, D), lambda i: (i, 0)),
            _resident((1, D)),
            _resident((1, D)),
            _resident((1, D)),
            _resident((D, D)),
            _resident((D, 2 * D_FF)),
            _resident((D_FF, D)),
            _resident((CONV_WIDTH, D_FF)),
            _resident((1, D_FF)),
        ],
        out_specs=pl.BlockSpec((tm, D), lambda i: (i, 0)),
        out_shape=jax.ShapeDtypeStruct((T, D), F32),
        scratch_shapes=[
            pltpu.VMEM((tm, D), F32),
            pltpu.VMEM((rb, D_FF), BF16),
            pltpu.VMEM((8, D_FF), F32),
            pltpu.VMEM((rb + 8, D_FF), F32),
            pltpu.VMEM((rb + 8, D_FF), F32),
            pltpu.VMEM((rb, D_FF), F32),
            pltpu.VMEM((rb, D_FF), F32),
        ],
        input_output_aliases={1: 0} if in_place else {},
        compiler_params=_params(1),
        name="post_ffn",
    )(o, x, g1.reshape(1, D), g2.reshape(1, D), g3.reshape(1, D), w_o.astype(BF16),
      w_up.astype(BF16), w_down.astype(BF16), conv_w, conv_b.reshape(1, D_FF))


def _dup_heads(w, n_heads):
    lead = w.shape[:-1]
    w = w.reshape(lead + (n_heads, 1, HEAD_DIM))
    return jnp.broadcast_to(w, lead + (n_heads, 2, HEAD_DIM)).reshape(lead + (2 * n_heads * HEAD_DIM,))


def kernel(x, norm_gains, w_qkv_a, lambda_qk_a, subln_a, w_o_a, kv_norm, w_kv_b, b_kv_b, w_q_b, b_q_b,
           sinks_b, w_o_b, w_up, conv_w, conv_b, w_down):
    B, S, D = x.shape
    T = B * S
    assert D == D_MODEL and S % ATTN_QUERY_TILE == 0 and S % ROW_TILE == 0 and S % FF_ROW_TILE == 0 and S >= 2 * WINDOW
    xt = x.reshape(T, D)
    kv3 = None
    for layer in range(DEPTH):
        g = norm_gains[layer]
        if layer < N_A_LAYERS:
            lambda_init = 0.8 - 0.6 * math.exp(-0.3 * layer)
            qkv3 = _proj(xt, g[0], w_qkv_a[layer].astype(BF16), jnp.zeros((3 * D,), F32),
                         n_scaled=DIFF_HEADS, scale=QK_SCALE * LOG2E)
            o = _attn_a(qkv3, lambda_qk_a[layer], subln_a[layer], lambda_init, B, S)
            w_o = w_o_a[layer]
        else:
            j = layer - N_A_LAYERS
            if kv3 is None:
                kvw = SWA_KV_HEADS * HEAD_DIM
                w_kv = jnp.concatenate([_dup_heads(w_kv_b[:, :kvw], SWA_KV_HEADS), w_kv_b[:, kvw:]], axis=1)
                b_kv = jnp.concatenate([_dup_heads(b_kv_b[:kvw], SWA_KV_HEADS), b_kv_b[kvw:]])
                kv3 = _proj(xt, kv_norm, w_kv.astype(BF16), b_kv)
            q3 = _proj(xt, g[0], w_q_b[j].astype(BF16), b_q_b[j],
                       n_scaled=SWA_Q_HEADS // 2, scale=QK_SCALE * LOG2E)
            o = _attn_b(q3, kv3, kv3, sinks_b[j], B, S)
            w_o = w_o_b[j]
        xt = _post_ffn(o, xt, g[1], g[2], g[3], w_o, w_up[layer], conv_w[layer], conv_b[layer],
                       w_down[layer], S, in_place=layer > 0)
    return xt.reshape(B, S, D)
```

```python
import functools
import math

import ml_dtypes
import numpy as np
import jax
import jax.numpy as jnp
from jax import lax
from jax.experimental import pallas as pl
from jax.experimental.pallas import tpu as pltpu

D_MODEL = 1024
HEAD_DIM = 64
DEPTH = 4
N_A_LAYERS = DEPTH // 2
DIFF_HEADS = D_MODEL // (2 * HEAD_DIM)
SWA_Q_HEADS = D_MODEL // HEAD_DIM
SWA_KV_HEADS = 2
SWA_GROUP = SWA_Q_HEADS // SWA_KV_HEADS
WINDOW = 128
D_FF = 2816
CONV_WIDTH = 3
NORM_EPS = 1e-6
SUBLN_EPS = 1e-5

LANES = 128
QK_SCALE = HEAD_DIM ** -0.5
VMEM_LIMIT_BYTES = 56 * 1024 * 1024
ROW_TILE = 512
ATTN_QUERY_TILE = 2048
ATTN_KEY_TILE = 512
ATTN_SCORES_AHEAD = 2
ATTN_SCORE_SLOTS = 4
SUM_ROWS = 16
SWA_BLOCKS_PER_STEP = 8
FF_CHUNK = 256
FF_DOT_COLS = 512
FF_ROW_TILE = 512
FF_ROW_BLOCK = 256
POS_PARTS = 3
LOG2E = math.log2(math.e)

BF16 = jnp.bfloat16
F32 = jnp.float32


def _alibi_slopes(n):
    return [2.0 ** (-8.0 * (i + 1) / n) for i in range(n)]


def _rms(x, g, eps):
    return x * lax.rsqrt(jnp.mean(x * x, axis=-1, keepdims=True) + eps) * g


def _params(n_axes):
    return pltpu.CompilerParams(
        dimension_semantics=("arbitrary",) * n_axes, vmem_limit_bytes=VMEM_LIMIT_BYTES)


def _resident(shape):
    nd = len(shape)
    return pl.BlockSpec(shape, lambda *_: (0,) * nd, pipeline_mode=pl.Buffered(1))


def _proj_kernel(x_ref, g_ref, w_ref, b_ref, o_ref, *, n_slabs, dot_slabs, n_scaled, scale):
    h = _rms(x_ref[...], g_ref[...], NORM_EPS).astype(BF16)
    for c0 in range(0, n_slabs, dot_slabs):
        nc = min(dot_slabs, n_slabs - c0)
        cols = slice(c0 * LANES, (c0 + nc) * LANES)
        y = jnp.dot(h, w_ref[:, cols], preferred_element_type=F32) + b_ref[:, cols]
        for c in range(nc):
            slab = y[:, c * LANES:(c + 1) * LANES]
            if c0 + c < n_scaled:
                slab = slab * scale
            o_ref[c0 + c] = slab.astype(BF16)


def _proj(x, g, w, b, n_scaled=0, scale=1.0):
    T, D = x.shape
    N = w.shape[1]
    n_slabs = N // LANES
    tm = ROW_TILE
    return pl.pallas_call(
        functools.partial(_proj_kernel, n_slabs=n_slabs, dot_slabs=4, n_scaled=n_scaled, scale=scale),
        grid=(T // tm,),
        in_specs=[
            pl.BlockSpec((tm, D), lambda i: (i, 0)),
            _resident((1, D)),
            _resident((D, N)),
            _resident((1, N)),
        ],
        out_specs=pl.BlockSpec((n_slabs, tm, LANES), lambda i: (0, i, 0)),
        out_shape=jax.ShapeDtypeStruct((n_slabs, T, LANES), BF16),
        compiler_params=_params(1),
        name="norm_proj",
    )(x, g.reshape(1, D), w, b.reshape(1, N))


def _attn_a_kernel(q_ref, k_ref, v_ref, kpos_ref, lq_ref, sg_ref, o_ref,
                   kk_scr, vt_scr, qq_scr, s_scr, m_scr, acc_scr, *, lambda_init):
    qi = pl.program_id(2)
    tq = q_ref.shape[0]
    tk = ATTN_KEY_TILE
    halves = tq // tk
    n_chunks = 2 * halves
    assert n_chunks % 2 == 0

    @pl.when(qi == 0)
    def _():
        kk_scr[:, :LANES] = k_ref[...]
        kk_scr[:, LANES:] = kpos_ref[...]
        for t in range(vt_scr.shape[0]):
            vt_scr[t, :LANES, :] = v_ref[t * tk:(t + 1) * tk, :].T
            vt_scr[t, LANES:, :] = jnp.ones((SUM_ROWS, tk), BF16)

    q = q_ref[...]
    lane = lax.broadcasted_iota(jnp.int32, (tq, LANES), 1)
    zero = jnp.zeros_like(q)
    ones_cols = jnp.where(lane < POS_PARTS, 1.0, 0.0).astype(BF16)
    qq_scr[:tq, :LANES] = jnp.where(lane < HEAD_DIM, q, zero)
    qq_scr[tq:, :LANES] = jnp.where(lane >= HEAD_DIM, q, zero)
    qq_scr[:tq, LANES:] = ones_cols
    qq_scr[tq:, LANES:] = ones_cols

    m_scr[...] = jnp.full(m_scr.shape, -jnp.inf, F32)
    acc_scr[...] = jnp.zeros(acc_scr.shape, F32)

    def scores(ki, c, slot):
        start = pl.multiple_of(ki * tk, tk)
        s_scr[slot] = lax.dot_general(
            kk_scr[pl.ds(start, tk), :], qq_scr[c * tk:(c + 1) * tk, :], (((1,), (1,)), ((), ())),
            preferred_element_type=F32)

    def softmax_pv(ki, c, slot, triangular):
        cols = slice(c * tk, (c + 1) * tk)
        s = s_scr[slot]
        if triangular:
            key = lax.broadcasted_iota(jnp.int32, (tk, tk), 0)
            qry = lax.broadcasted_iota(jnp.int32, (tk, tk), 1)
            s = jnp.where(key <= qry, s, -jnp.inf)
        m_prev = m_scr[:, cols]
        m_new = jnp.maximum(m_prev, jnp.max(s, axis=0, keepdims=True))
        alpha = jnp.exp2(m_prev - m_new)
        p = jnp.exp2(s - m_new)
        acc_scr[:, cols] = alpha * acc_scr[:, cols] + jnp.dot(
            vt_scr[ki], p.astype(BF16), preferred_element_type=F32)
        m_scr[:, cols] = m_new

    ahead = ATTN_SCORES_AHEAD
    n_slots = s_scr.shape[0]
    assert n_chunks % n_slots == 0 and ahead < n_slots and ahead <= n_chunks
    first_diag = halves * qi
    items = [(d, c) for d in range(halves) for c in range(n_chunks) if c % halves >= d]
    assert items[:ahead] == [(0, c) for c in range(ahead)]

    for c in range(ahead):
        scores(0, c, c)

    def body(ki, carry):
        for c in range(n_chunks):
            nxt = c + ahead
            scores(ki + nxt // n_chunks, nxt % n_chunks, nxt % n_slots)
            softmax_pv(ki, c, c % n_slots, False)
        return carry

    lax.fori_loop(0, first_diag, body, 0)

    for n, (d, c) in enumerate(items):
        if n + ahead < len(items):
            d_next, c_next = items[n + ahead]
            scores(first_diag + d_next, c_next, (n + ahead) % n_slots)
        softmax_pv(first_diag + d, c, n % n_slots, c % halves == d)

    lq = lq_ref[...]
    lam = (jnp.exp(jnp.sum(lq[0:1] * lq[1:2], axis=-1, keepdims=True))
           - jnp.exp(jnp.sum(lq[2:3] * lq[3:4], axis=-1, keepdims=True)) + lambda_init)
    l = acc_scr[LANES:LANES + 1, :]
    o = (acc_scr[:LANES, :tq] / l[:, :tq] - lam * (acc_scr[:LANES, tq:] / l[:, tq:]))
    o = o * lax.rsqrt(jnp.mean(o * o, axis=0, keepdims=True) + SUBLN_EPS) * sg_ref[...]
    o_ref[...] = (o * (1.0 - lambda_init)).T.astype(BF16)


def _alibi_key_columns(S):
    j = np.arange(S, dtype=np.float64)
    cols = np.zeros((DIFF_HEADS, S, LANES), np.float32)
    for h, slope in enumerate(_alibi_slopes(DIFF_HEADS)):
        rest = slope * LOG2E * j
        for part in range(POS_PARTS):
            piece = rest.astype(ml_dtypes.bfloat16).astype(np.float64)
            cols[h, :, part] = piece
            rest = rest - piece
    return jnp.asarray(cols, BF16)


def _attn_a(qkv3, lam_qk, subln_g, lambda_init, B, S):
    T = B * S
    H = DIFF_HEADS
    tq = ATTN_QUERY_TILE
    tk = ATTN_KEY_TILE
    nq = S // tq
    kernel = functools.partial(_attn_a_kernel, lambda_init=lambda_init)
    return pl.pallas_call(
        kernel,
        grid=(B, H, nq),
        in_specs=[
            pl.BlockSpec((None, tq, LANES), lambda b, h, qi: (h, b * nq + qi, 0)),
            pl.BlockSpec((None, S, LANES), lambda b, h, qi: (H + h, b, 0)),
            pl.BlockSpec((None, S, LANES), lambda b, h, qi: (2 * H + h, b, 0)),
            pl.BlockSpec((None, S, LANES), lambda b, h, qi: (h, 0, 0)),
            _resident((4, HEAD_DIM)),
            _resident((LANES, 1)),
        ],
        out_specs=pl.BlockSpec((tq, LANES), lambda b, h, qi: (b * nq + qi, h)),
        out_shape=jax.ShapeDtypeStruct((T, D_MODEL), BF16),
        scratch_shapes=[
            pltpu.VMEM((S, 2 * LANES), BF16),
            pltpu.VMEM((S // tk, LANES + SUM_ROWS, tk), BF16),
            pltpu.VMEM((2 * tq, 2 * LANES), BF16),
            pltpu.VMEM((ATTN_SCORE_SLOTS, tk, tk), F32),
            pltpu.VMEM((1, 2 * tq), F32),
            pltpu.VMEM((LANES + SUM_ROWS, 2 * tq), F32),
        ],
        compiler_params=_params(3),
        name="diff_attn",
    )(qkv3, qkv3, qkv3, _alibi_key_columns(S), lam_qk, subln_g.reshape(LANES, 1))


def _attn_b_kernel(sinks_ref, q_ref, k_ref, v_ref, bias_ref, o_ref):
    lane = lax.broadcasted_iota(jnp.int32, (WINDOW, LANES), 1)
    pairs_per_kv = SWA_GROUP // 2
    group_lanes = SWA_GROUP * WINDOW
    ones_rows = jnp.ones((SUM_ROWS, 2 * WINDOW), BF16)

    def window_start(qb):
        nb = pl.program_id(1) * SWA_BLOCKS_PER_STEP + qb
        return nb, pl.multiple_of(jnp.maximum(nb - 1, 0) * WINDOW, WINDOW)

    def scores(qb, kh):
        _, start = window_start(qb)
        kwin = k_ref[kh, pl.ds(start, 2 * WINDOW), :]
        rows = []
        for cpair in range(pairs_per_kv):
            qc = q_ref[kh * pairs_per_kv + cpair, qb * WINDOW:(qb + 1) * WINDOW, :]
            zero = jnp.zeros_like(qc)
            rows.append(jnp.where(lane < HEAD_DIM, qc, zero))
            rows.append(jnp.where(lane >= HEAD_DIM, qc, zero))
        return lax.dot_general(kwin, jnp.concatenate(rows, axis=0), (((1,), (1,)), ((), ())),
                               preferred_element_type=F32)

    def softmax_pv(qb, kh, s):
        nb, start = window_start(qb)
        bias = bias_ref.at[jnp.minimum(nb, 1) if qb == 0 else 1]
        s = s + bias[:, kh * group_lanes:(kh + 1) * group_lanes]
        sink = jnp.concatenate(
            [jnp.full((1, WINDOW), sinks_ref[kh * SWA_GROUP + g] * LOG2E, F32) for g in range(SWA_GROUP)],
            axis=1)
        m = jnp.maximum(jnp.max(s, axis=0, keepdims=True), sink)
        p = jnp.exp2(s - m).astype(BF16)
        vt = v_ref[pl.ds(start, 2 * WINDOW), :].T[kh * HEAD_DIM:(kh + 1) * HEAD_DIM]
        o = jnp.dot(jnp.concatenate([vt, ones_rows], axis=0), p,
                    preferred_element_type=F32)
        denom = o[HEAD_DIM:HEAD_DIM + 1] + jnp.exp2(sink - m)
        o = o[:HEAD_DIM] * (1.0 / denom)
        for cpair in range(pairs_per_kv):
            pair = jnp.concatenate([o[:, (2 * cpair) * WINDOW:(2 * cpair + 1) * WINDOW],
                                    o[:, (2 * cpair + 1) * WINDOW:(2 * cpair + 2) * WINDOW]], axis=0)
            col = (kh * pairs_per_kv + cpair) * LANES
            o_ref[qb * WINDOW:(qb + 1) * WINDOW, col:col + LANES] = pair.T.astype(BF16)

    items = [(qb, kh) for qb in range(SWA_BLOCKS_PER_STEP) for kh in range(SWA_KV_HEADS)]
    s_next = scores(*items[0])
    for n, item in enumerate(items):
        s_cur = s_next
        if n + 1 < len(items):
            s_next = scores(*items[n + 1])
        softmax_pv(*item, s_cur)


def _swa_bias():
    key = np.arange(2 * WINDOW)[:, None]
    qry = np.arange(WINDOW)[None, :]
    bias = np.full((2, 2 * WINDOW, SWA_Q_HEADS * WINDOW), -np.inf, np.float32)
    for case, offset in enumerate((0, WINDOW)):
        dist = offset + qry - key
        valid = (dist >= 0) & (dist < WINDOW)
        for h, slope in enumerate(_alibi_slopes(SWA_Q_HEADS)):
            bias[case, :, h * WINDOW:(h + 1) * WINDOW] = np.where(valid, -slope * LOG2E * dist, -np.inf)
    return jnp.asarray(bias)


def _attn_b(q3, k3, v3, sinks, B, S):
    T = B * S
    rows = SWA_BLOCKS_PER_STEP * WINDOW
    nstep = S // rows
    n_pairs = SWA_Q_HEADS // 2
    return pl.pallas_call(
        _attn_b_kernel,
        grid_spec=pltpu.PrefetchScalarGridSpec(
            num_scalar_prefetch=1,
            grid=(B, nstep),
            in_specs=[
                pl.BlockSpec((n_pairs, rows, LANES), lambda b, n, *_: (0, b * nstep + n, 0)),
                pl.BlockSpec((SWA_KV_HEADS, S, LANES), lambda b, n, *_: (0, b, 0)),
                pl.BlockSpec((None, S, LANES), lambda b, n, *_: (SWA_KV_HEADS, b, 0)),
                _resident((2, 2 * WINDOW, SWA_Q_HEADS * WINDOW)),
            ],
            out_specs=pl.BlockSpec((rows, D_MODEL), lambda b, n, *_: (b * nstep + n, 0)),
        ),
        out_shape=jax.ShapeDtypeStruct((T, D_MODEL), BF16),
        compiler_params=_params(2),
        name="swa_attn",
    )(sinks, q3, k3, v3, _swa_bias())


def _gelu_tanh(x):
    return 0.5 * x * (1.0 + jnp.tanh(math.sqrt(2.0 / math.pi) * (x + 0.044715 * (x * x * x))))


def _ffn_kernel(o_ref, x_ref, g1_ref, g2_ref, g3_ref, wo_ref, wgu_ref, wd_ref, cw_ref, cb_ref, out_ref,
                x1_scr, a_scr, carry_scr, gate0_scr, gate1_scr, up0_scr, up1_scr, *, tiles_per_seq):
    i = pl.program_id(0)
    tm = x_ref.shape[0]
    rb = FF_ROW_BLOCK
    n_blocks = tm // rb
    tf = FF_CHUNK
    slots = ((gate0_scr, up0_scr), (gate1_scr, up1_scr))

    @pl.when(i == 0)
    def _():
        carry_scr[...] = jnp.zeros(carry_scr.shape, F32)

    seq_start = (i % tiles_per_seq) == 0

    def up_proj(r):
        gate_scr, up_scr = slots[r % 2]
        rows = slice(r * rb, (r + 1) * rb)
        if r == 0:
            gate_scr[0:8, :] = jnp.where(seq_start, 0.0, carry_scr[...])
        else:
            gate_scr[0:8, :] = slots[(r - 1) % 2][0][rb:rb + 8, :]
        mix = jnp.dot(o_ref[rows, :], wo_ref[...], preferred_element_type=F32)
        x1 = x_ref[rows, :] + _rms(mix, g1_ref[...], NORM_EPS)
        x1_scr[rows, :] = x1
        h = _rms(x1, g2_ref[...], NORM_EPS).astype(BF16)
        for c in range(0, D_FF, FF_DOT_COLS):
            w = min(FF_DOT_COLS, D_FF - c)
            gate_scr[8:8 + rb, c:c + w] = jnp.dot(h, wgu_ref[:, c:c + w], preferred_element_type=F32)
            up_scr[:, c:c + w] = jnp.dot(h, wgu_ref[:, D_FF + c:D_FF + c + w],
                                         preferred_element_type=F32)

    def act_down(r):
        gate_scr, up_scr = slots[r % 2]
        rows = slice(r * rb, (r + 1) * rb)
        for c in range(0, D_FF, tf):
            cols = slice(c, c + tf)
            conv = (cw_ref[2:3, cols] * gate_scr[8:8 + rb, cols] + cb_ref[:, cols]
                    + cw_ref[1:2, cols] * gate_scr[7:7 + rb, cols]
                    + cw_ref[0:1, cols] * gate_scr[6:6 + rb, cols])
            a_scr[:, cols] = (_gelu_tanh(conv) * up_scr[:, cols]).astype(BF16)
        f = jnp.dot(a_scr[...], wd_ref[...], preferred_element_type=F32)
        out_ref[rows, :] = x1_scr[rows, :] + _rms(f, g3_ref[...], NORM_EPS)
        if r == n_blocks - 1:
            carry_scr[...] = gate_scr[rb:rb + 8, :]

    up_proj(0)
    for r in range(n_blocks):
        if r + 1 < n_blocks:
            up_proj(r + 1)
        act_down(r)


def _post_ffn(o, x, g1, g2, g3, w_o, w_up, conv_w, conv_b, w_down, S, in_place):
    T, D = x.shape
    tm = FF_ROW_TILE
    rb = FF_ROW_BLOCK
    kernel = functools.partial(_ffn_kernel, tiles_per_seq=S // tm)
    return pl.pallas_call(
        kernel,
        grid=(T // tm,),
        in_specs=[
            pl.BlockSpec((tm, D), lambda i: (i, 0)),
            pl.BlockSpec((tm, D), lambda i: (i, 0)),
            _resident((1, D)),
            _resident((1, D)),
            _resident((1, D)),
            _resident((D, D)),
            _resident((D, 2 * D_FF)),
            _resident((D_FF, D)),
            _resident((CONV_WIDTH, D_FF)),
            _resident((1, D_FF)),
        ],
        out_specs=pl.BlockSpec((tm, D), lambda i: (i, 0)),
        out_shape=jax.ShapeDtypeStruct((T, D), F32),
        scratch_shapes=[
            pltpu.VMEM((tm, D), F32),
            pltpu.VMEM((rb, D_FF), BF16),
            pltpu.VMEM((8, D_FF), F32),
            pltpu.VMEM((rb + 8, D_FF), F32),
            pltpu.VMEM((rb + 8, D_FF), F32),
            pltpu.VMEM((rb, D_FF), F32),
            pltpu.VMEM((rb, D_FF), F32),
        ],
        input_output_aliases={1: 0} if in_place else {},
        compiler_params=_params(1),
        name="post_ffn",
    )(o, x, g1.reshape(1, D), g2.reshape(1, D), g3.reshape(1, D), w_o.astype(BF16),
      w_up.astype(BF16), w_down.astype(BF16), conv_w, conv_b.reshape(1, D_FF))


def _dup_heads(w, n_heads):
    lead = w.shape[:-1]
    w = w.reshape(lead + (n_heads, 1, HEAD_DIM))
    return jnp.broadcast_to(w, lead + (n_heads, 2, HEAD_DIM)).reshape(lead + (2 * n_heads * HEAD_DIM,))


def kernel(x, norm_gains, w_qkv_a, lambda_qk_a, subln_a, w_o_a, kv_norm, w_kv_b, b_kv_b, w_q_b, b_q_b,
           sinks_b, w_o_b, w_up, conv_w, conv_b, w_down):
    B, S, D = x.shape
    T = B * S
    assert D == D_MODEL and S % ATTN_QUERY_TILE == 0 and S % ROW_TILE == 0 and S % FF_ROW_TILE == 0 and S >= 2 * WINDOW
    xt = x.reshape(T, D)
    kv3 = None
    for layer in range(DEPTH):
        g = norm_gains[layer]
        if layer < N_A_LAYERS:
            lambda_init = 0.8 - 0.6 * math.exp(-0.3 * layer)
            qkv3 = _proj(xt, g[0], w_qkv_a[layer].astype(BF16), jnp.zeros((3 * D,), F32),
                         n_scaled=DIFF_HEADS, scale=QK_SCALE * LOG2E)
            o = _attn_a(qkv3, lambda_qk_a[layer], subln_a[layer], lambda_init, B, S)
            w_o = w_o_a[layer]
        else:
            j = layer - N_A_LAYERS
            if kv3 is None:
                kvw = SWA_KV_HEADS * HEAD_DIM
                w_kv = jnp.concatenate([_dup_heads(w_kv_b[:, :kvw], SWA_KV_HEADS), w_kv_b[:, kvw:]], axis=1)
                b_kv = jnp.concatenate([_dup_heads(b_kv_b[:kvw], SWA_KV_HEADS), b_kv_b[kvw:]])
                kv3 = _proj(xt, kv_norm, w_kv.astype(BF16), b_kv)
            q3 = _proj(xt, g[0], w_q_b[j].astype(BF16), b_q_b[j],
                       n_scaled=SWA_Q_HEADS // 2, scale=QK_SCALE * LOG2E)
            o = _attn_b(q3, kv3, kv3, sinks_b[j], B, S)
            w_o = w_o_b[j]
        xt = _post_ffn(o, xt, g[1], g[2], g[3], w_o, w_up[layer], conv_w[layer], conv_b[layer],
                       w_down[layer], S, in_place=layer > 0)
    return xt.reshape(B, S, D)
```

```python
import functools
import math

import ml_dtypes
import numpy as np
import jax
import jax.numpy as jnp
from jax import lax
from jax.experimental import pallas as pl
from jax.experimental.pallas import tpu as pltpu

D_MODEL = 1024
HEAD_DIM = 64
DEPTH = 4
N_A_LAYERS = DEPTH // 2
DIFF_HEADS = D_MODEL // (2 * HEAD_DIM)
SWA_Q_HEADS = D_MODEL // HEAD_DIM
SWA_KV_HEADS = 2
SWA_GROUP = SWA_Q_HEADS // SWA_KV_HEADS
WINDOW = 128
D_FF = 2816
CONV_WIDTH = 3
NORM_EPS = 1e-6
SUBLN_EPS = 1e-5

LANES = 128
QK_SCALE = HEAD_DIM ** -0.5
VMEM_LIMIT_BYTES = 56 * 1024 * 1024
ROW_TILE = 512
ATTN_QUERY_TILE = 2048
ATTN_KEY_TILE = 512
ATTN_SCORES_AHEAD = 1
ATTN_SCORE_SLOTS = 2
SUM_ROWS = 16
SWA_BLOCKS_PER_STEP = 8
FF_CHUNK = 256
FF_DOT_COLS = 256
FF_ROW_TILE = 512
FF_ROW_BLOCK = 256
POS_PARTS = 3
LOG2E = math.log2(math.e)

BF16 = jnp.bfloat16
F32 = jnp.float32


def _alibi_slopes(n):
    return [2.0 ** (-8.0 * (i + 1) / n) for i in range(n)]


def _rms(x, g, eps):
    return x * lax.rsqrt(jnp.mean(x * x, axis=-1, keepdims=True) + eps) * g


def _params(n_axes):
    return pltpu.CompilerParams(
        dimension_semantics=("arbitrary",) * n_axes, vmem_limit_bytes=VMEM_LIMIT_BYTES)


def _resident(shape):
    nd = len(shape)
    return pl.BlockSpec(shape, lambda *_: (0,) * nd, pipeline_mode=pl.Buffered(1))


def _proj_kernel(x_ref, g_ref, w_ref, b_ref, o_ref, *, n_slabs, dot_slabs, n_scaled, scale):
    h = _rms(x_ref[...], g_ref[...], NORM_EPS).astype(BF16)
    for c0 in range(0, n_slabs, dot_slabs):
        nc = min(dot_slabs, n_slabs - c0)
        cols = slice(c0 * LANES, (c0 + nc) * LANES)
        y = jnp.dot(h, w_ref[:, cols], preferred_element_type=F32) + b_ref[:, cols]
        for c in range(nc):
            slab = y[:, c * LANES:(c + 1) * LANES]
            if c0 + c < n_scaled:
                slab = slab * scale
            o_ref[c0 + c] = slab.astype(BF16)


def _proj(x, g, w, b, n_scaled=0, scale=1.0):
    T, D = x.shape
    N = w.shape[1]
    n_slabs = N // LANES
    tm = ROW_TILE
    return pl.pallas_call(
        functools.partial(_proj_kernel, n_slabs=n_slabs, dot_slabs=4, n_scaled=n_scaled, scale=scale),
        grid=(T // tm,),
        in_specs=[
            pl.BlockSpec((tm, D), lambda i: (i, 0)),
            _resident((1, D)),
            _resident((D, N)),
            _resident((1, N)),
        ],
        out_specs=pl.BlockSpec((n_slabs, tm, LANES), lambda i: (0, i, 0)),
        out_shape=jax.ShapeDtypeStruct((n_slabs, T, LANES), BF16),
        compiler_params=_params(1),
        name="norm_proj",
    )(x, g.reshape(1, D), w, b.reshape(1, N))


def _attn_a_kernel(q_ref, k_ref, v_ref, kpos_ref, lq_ref, sg_ref, o_ref,
                   kk_scr, vt_scr, qq_scr, s_scr, m_scr, acc_scr, *, lambda_init):
    qi = pl.program_id(2)
    tq = q_ref.shape[0]
    tk = ATTN_KEY_TILE
    halves = tq // tk
    n_chunks = 2 * halves
    assert n_chunks % 2 == 0

    @pl.when(qi == 0)
    def _():
        kk_scr[:, :LANES] = k_ref[...]
        kk_scr[:, LANES:] = kpos_ref[...]
        for t in range(vt_scr.shape[0]):
            vt_scr[t, :LANES, :] = v_ref[t * tk:(t + 1) * tk, :].T
            vt_scr[t, LANES:, :] = jnp.ones((SUM_ROWS, tk), BF16)

    q = q_ref[...]
    lane = lax.broadcasted_iota(jnp.int32, (tq, LANES), 1)
    zero = jnp.zeros_like(q)
    ones_cols = jnp.where(lane < POS_PARTS, 1.0, 0.0).astype(BF16)
    qq_scr[:tq, :LANES] = jnp.where(lane < HEAD_DIM, q, zero)
    qq_scr[tq:, :LANES] = jnp.where(lane >= HEAD_DIM, q, zero)
    qq_scr[:tq, LANES:] = ones_cols
    qq_scr[tq:, LANES:] = ones_cols

    m_scr[...] = jnp.full(m_scr.shape, -jnp.inf, F32)
    acc_scr[...] = jnp.zeros(acc_scr.shape, F32)

    def scores(ki, c, slot):
        start = pl.multiple_of(ki * tk, tk)
        s_scr[slot] = lax.dot_general(
            kk_scr[pl.ds(start, tk), :], qq_scr[c * tk:(c + 1) * tk, :], (((1,), (1,)), ((), ())),
            preferred_element_type=F32)

    def softmax_pv(ki, c, slot, triangular):
        cols = slice(c * tk, (c + 1) * tk)
        s = s_scr[slot]
        if triangular:
            key = lax.broadcasted_iota(jnp.int32, (tk, tk), 0)
            qry = lax.broadcasted_iota(jnp.int32, (tk, tk), 1)
            s = jnp.where(key <= qry, s, -jnp.inf)
        m_prev = m_scr[:, cols]
        m_new = jnp.maximum(m_prev, jnp.max(s, axis=0, keepdims=True))
        alpha = jnp.exp2(m_prev - m_new)
        p = jnp.exp2(s - m_new)
        acc_scr[:, cols] = alpha * acc_scr[:, cols] + jnp.dot(
            vt_scr[ki], p.astype(BF16), preferred_element_type=F32)
        m_scr[:, cols] = m_new

    ahead = ATTN_SCORES_AHEAD
    n_slots = s_scr.shape[0]
    assert n_chunks % n_slots == 0 and ahead < n_slots and ahead <= n_chunks
    first_diag = halves * qi
    items = [(d, c) for d in range(halves) for c in range(n_chunks) if c % halves >= d]
    assert items[:ahead] == [(0, c) for c in range(ahead)]

    for c in range(ahead):
        scores(0, c, c)

    def body(ki, carry):
        for c in range(n_chunks):
            nxt = c + ahead
            scores(ki + nxt // n_chunks, nxt % n_chunks, nxt % n_slots)
            softmax_pv(ki, c, c % n_slots, False)
        return carry

    def two_tiles(kp, carry):
        return body(2 * kp + 1, body(2 * kp, carry))

    assert halves % 2 == 0
    lax.fori_loop(0, first_diag // 2, two_tiles, 0)

    for n, (d, c) in enumerate(items):
        if n + ahead < len(items):
            d_next, c_next = items[n + ahead]
            scores(first_diag + d_next, c_next, (n + ahead) % n_slots)
        softmax_pv(first_diag + d, c, n % n_slots, c % halves == d)

    lq = lq_ref[...]
    lam = (jnp.exp(jnp.sum(lq[0:1] * lq[1:2], axis=-1, keepdims=True))
           - jnp.exp(jnp.sum(lq[2:3] * lq[3:4], axis=-1, keepdims=True)) + lambda_init)
    l = acc_scr[LANES:LANES + 1, :]
    o = (acc_scr[:LANES, :tq] / l[:, :tq] - lam * (acc_scr[:LANES, tq:] / l[:, tq:]))
    o = o * lax.rsqrt(jnp.mean(o * o, axis=0, keepdims=True) + SUBLN_EPS) * sg_ref[...]
    o_ref[...] = (o * (1.0 - lambda_init)).T.astype(BF16)


def _alibi_key_columns(S):
    j = np.arange(S, dtype=np.float64)
    cols = np.zeros((DIFF_HEADS, S, LANES), np.float32)
    for h, slope in enumerate(_alibi_slopes(DIFF_HEADS)):
        rest = slope * LOG2E * j
        for part in range(POS_PARTS):
            piece = rest.astype(ml_dtypes.bfloat16).astype(np.float64)
            cols[h, :, part] = piece
            rest = rest - piece
    return jnp.asarray(cols, BF16)


def _attn_a(qkv3, lam_qk, subln_g, lambda_init, B, S):
    T = B * S
    H = DIFF_HEADS
    tq = ATTN_QUERY_TILE
    tk = ATTN_KEY_TILE
    nq = S // tq
    kernel = functools.partial(_attn_a_kernel, lambda_init=lambda_init)
    return pl.pallas_call(
        kernel,
        grid=(B, H, nq),
        in_specs=[
            pl.BlockSpec((None, tq, LANES), lambda b, h, qi: (h, b * nq + qi, 0)),
            pl.BlockSpec((None, S, LANES), lambda b, h, qi: (H + h, b, 0)),
            pl.BlockSpec((None, S, LANES), lambda b, h, qi: (2 * H + h, b, 0)),
            pl.BlockSpec((None, S, LANES), lambda b, h, qi: (h, 0, 0)),
            _resident((4, HEAD_DIM)),
            _resident((LANES, 1)),
        ],
        out_specs=pl.BlockSpec((tq, LANES), lambda b, h, qi: (b * nq + qi, h)),
        out_shape=jax.ShapeDtypeStruct((T, D_MODEL), BF16),
        scratch_shapes=[
            pltpu.VMEM((S, 2 * LANES), BF16),
            pltpu.VMEM((S // tk, LANES + SUM_ROWS, tk), BF16),
            pltpu.VMEM((2 * tq, 2 * LANES), BF16),
            pltpu.VMEM((ATTN_SCORE_SLOTS, tk, tk), F32),
            pltpu.VMEM((1, 2 * tq), F32),
            pltpu.VMEM((LANES + SUM_ROWS, 2 * tq), F32),
        ],
        compiler_params=_params(3),
        name="diff_attn",
    )(qkv3, qkv3, qkv3, _alibi_key_columns(S), lam_qk, subln_g.reshape(LANES, 1))


def _attn_b_kernel(sinks_ref, q_ref, k_ref, v_ref, bias_ref, o_ref, s_scr):
    lane = lax.broadcasted_iota(jnp.int32, (WINDOW, LANES), 1)
    pairs_per_kv = SWA_GROUP // 2
    group_lanes = SWA_GROUP * WINDOW
    ones_rows = jnp.ones((SUM_ROWS, 2 * WINDOW), BF16)

    def window_start(qb):
        nb = pl.program_id(1) * SWA_BLOCKS_PER_STEP + qb
        return nb, pl.multiple_of(jnp.maximum(nb - 1, 0) * WINDOW, WINDOW)

    def scores(qb, kh):
        _, start = window_start(qb)
        kwin = k_ref[kh, pl.ds(start, 2 * WINDOW), :]
        rows = []
        for cpair in range(pairs_per_kv):
            qc = q_ref[kh * pairs_per_kv + cpair, qb * WINDOW:(qb + 1) * WINDOW, :]
            zero = jnp.zeros_like(qc)
            rows.append(jnp.where(lane < HEAD_DIM, qc, zero))
            rows.append(jnp.where(lane >= HEAD_DIM, qc, zero))
        return lax.dot_general(kwin, jnp.concatenate(rows, axis=0), (((1,), (1,)), ((), ())),
                               preferred_element_type=F32)

    def softmax_pv(qb, kh, s_ref):
        nb, start = window_start(qb)
        bias = bias_ref.at[jnp.minimum(nb, 1) if qb == 0 else 1]
        vt = v_ref[pl.ds(start, 2 * WINDOW), :].T[kh * HEAD_DIM:(kh + 1) * HEAD_DIM]
        v_ext = jnp.concatenate([vt, ones_rows], axis=0)
        for cpair in range(pairs_per_kv):
            lanes = slice(2 * cpair * WINDOW, (2 * cpair + 2) * WINDOW)
            sink = jnp.concatenate(
                [jnp.full((1, WINDOW), sinks_ref[kh * SWA_GROUP + 2 * cpair + g] * LOG2E, F32)
                 for g in range(2)], axis=1)
            b0 = kh * group_lanes + 2 * cpair * WINDOW
            s = s_ref[:, lanes] + bias[:, b0:b0 + 2 * WINDOW]
            m = jnp.maximum(jnp.max(s, axis=0, keepdims=True), sink)
            p = jnp.exp2(s - m).astype(BF16)
            o = jnp.dot(v_ext, p, preferred_element_type=F32)
            denom = o[HEAD_DIM:HEAD_DIM + 1] + jnp.exp2(sink - m)
            o = o[:HEAD_DIM] * (1.0 / denom)
            pair = jnp.concatenate([o[:, :WINDOW], o[:, WINDOW:]], axis=0)
            col = (kh * pairs_per_kv + cpair) * LANES
            o_ref[qb * WINDOW:(qb + 1) * WINDOW, col:col + LANES] = pair.T.astype(BF16)

    items = [(qb, kh) for qb in range(SWA_BLOCKS_PER_STEP) for kh in range(SWA_KV_HEADS)]
    s_scr[0] = scores(*items[0])
    for n, item in enumerate(items):
        if n + 1 < len(items):
            s_scr[(n + 1) % 2] = scores(*items[n + 1])
        softmax_pv(*item, s_scr.at[n % 2])


def _swa_bias():
    key = np.arange(2 * WINDOW)[:, None]
    qry = np.arange(WINDOW)[None, :]
    bias = np.full((2, 2 * WINDOW, SWA_Q_HEADS * WINDOW), -np.inf, np.float32)
    for case, offset in enumerate((0, WINDOW)):
        dist = offset + qry - key
        valid = (dist >= 0) & (dist < WINDOW)
        for h, slope in enumerate(_alibi_slopes(SWA_Q_HEADS)):
            bias[case, :, h * WINDOW:(h + 1) * WINDOW] = np.where(valid, -slope * LOG2E * dist, -np.inf)
    return jnp.asarray(bias)


def _attn_b(q3, k3, v3, sinks, B, S):
    T = B * S
    rows = SWA_BLOCKS_PER_STEP * WINDOW
    nstep = S // rows
    n_pairs = SWA_Q_HEADS // 2
    return pl.pallas_call(
        _attn_b_kernel,
        grid_spec=pltpu.PrefetchScalarGridSpec(
            num_scalar_prefetch=1,
            grid=(B, nstep),
            in_specs=[
                pl.BlockSpec((n_pairs, rows, LANES), lambda b, n, *_: (0, b * nstep + n, 0)),
                pl.BlockSpec((SWA_KV_HEADS, S, LANES), lambda b, n, *_: (0, b, 0)),
                pl.BlockSpec((None, S, LANES), lambda b, n, *_: (SWA_KV_HEADS, b, 0)),
                _resident((2, 2 * WINDOW, SWA_Q_HEADS * WINDOW)),
            ],
            out_specs=pl.BlockSpec((rows, D_MODEL), lambda b, n, *_: (b * nstep + n, 0)),
            scratch_shapes=[pltpu.VMEM((2, 2 * WINDOW, SWA_GROUP * WINDOW), F32)],
        ),
        out_shape=jax.ShapeDtypeStruct((T, D_MODEL), BF16),
        compiler_params=_params(2),
        name="swa_attn",
    )(sinks, q3, k3, v3, _swa_bias())


def _gelu_tanh(x):
    return 0.5 * x * (1.0 + jnp.tanh(math.sqrt(2.0 / math.pi) * (x + 0.044715 * (x * x * x))))


def _ffn_kernel(o_ref, x_ref, g1_ref, g2_ref, g3_ref, wo_ref, wgu_ref, wd_ref, cw_ref, cb_ref, out_ref,
                x1_scr, a_scr, carry_scr, gate0_scr, gate1_scr, up0_scr, up1_scr, *, tiles_per_seq):
    i = pl.program_id(0)
    tm = x_ref.shape[0]
    rb = FF_ROW_BLOCK
    n_blocks = tm // rb
    tf = FF_CHUNK
    slots = ((gate0_scr, up0_scr), (gate1_scr, up1_scr))

    @pl.when(i == 0)
    def _():
        carry_scr[...] = jnp.zeros(carry_scr.shape, F32)

    seq_start = (i % tiles_per_seq) == 0

    def up_proj(r):
        gate_scr, up_scr = slots[r % 2]
        rows = slice(r * rb, (r + 1) * rb)
        if r == 0:
            gate_scr[0:8, :] = jnp.where(seq_start, 0.0, carry_scr[...])
        else:
            gate_scr[0:8, :] = slots[(r - 1) % 2][0][rb:rb + 8, :]
        mix = jnp.dot(o_ref[rows, :], wo_ref[...], preferred_element_type=F32)
        x1 = x_ref[rows, :] + _rms(mix, g1_ref[...], NORM_EPS)
        x1_scr[rows, :] = x1
        h = _rms(x1, g2_ref[...], NORM_EPS).astype(BF16)
        for c in range(0, D_FF, FF_DOT_COLS):
            w = min(FF_DOT_COLS, D_FF - c)
            gate_scr[8:8 + rb, c:c + w] = jnp.dot(h, wgu_ref[:, c:c + w], preferred_element_type=F32)
            up_scr[:, c:c + w] = jnp.dot(h, wgu_ref[:, D_FF + c:D_FF + c + w],
                                         preferred_element_type=F32)

    def act_down(r):
        gate_scr, up_scr = slots[r % 2]
        rows = slice(r * rb, (r + 1) * rb)
        for c in range(0, D_FF, tf):
            cols = slice(c, c + tf)
            conv = (cw_ref[2:3, cols] * gate_scr[8:8 + rb, cols] + cb_ref[:, cols]
                    + cw_ref[1:2, cols] * gate_scr[7:7 + rb, cols]
                    + cw_ref[0:1, cols] * gate_scr[6:6 + rb, cols])
            a_scr[:, cols] = (_gelu_tanh(conv) * up_scr[:, cols]).astype(BF16)
        f = jnp.dot(a_scr[...], wd_ref[...], preferred_element_type=F32)
        out_ref[rows, :] = x1_scr[rows, :] + _rms(f, g3_ref[...], NORM_EPS)
        if r == n_blocks - 1:
            carry_scr[...] = gate_scr[rb:rb + 8, :]

    up_proj(0)
    for r in range(n_blocks):
        if r + 1 < n_blocks:
            up_proj(r + 1)
        act_down(r)


def _post_ffn(o, x, g1, g2, g3, w_o, w_up, conv_w, conv_b, w_down, S, in_place):
    T, D = x.shape
    tm = FF_ROW_TILE
    rb = FF_ROW_BLOCK
    kernel = functools.partial(_ffn_kernel, tiles_per_seq=S // tm)
    return pl.pallas_call(
        kernel,
        grid=(T // tm,),
        in_specs=[
            pl.BlockSpec((tm, D), lambda i: (i, 0)),
            pl.BlockSpec((tm, D), lambda i: (i, 0)),
            _resident((1, D)),
            _resident((1, D)),
            _resident((1, D)),
            _resident((D, D)),
            _resident((D, 2 * D_FF)),
            _resident((D_FF, D)),
            _resident((CONV_WIDTH, D_FF)),
            _resident((1, D_FF)),
        ],
        out_specs=pl.BlockSpec((tm, D), lambda i: (i, 0)),
        out_shape=jax.ShapeDtypeStruct((T, D), F32),
        scratch_shapes=[
            pltpu.VMEM((tm, D), F32),
            pltpu.VMEM((rb, D_FF), BF16),
            pltpu.VMEM((8, D_FF), F32),
            pltpu.VMEM((rb + 8, D_FF), F32),
            pltpu.VMEM((rb + 8, D_FF), F32),
            pltpu.VMEM((rb, D_FF), F32),
            pltpu.VMEM((rb, D_FF), F32),
        ],
        input_output_aliases={1: 0} if in_place else {},
        compiler_params=_params(1),
        name="post_ffn",
    )(o, x, g1.reshape(1, D), g2.reshape(1, D), g3.reshape(1, D), w_o.astype(BF16),
      w_up.astype(BF16), w_down.astype(BF16), conv_w, conv_b.reshape(1, D_FF))


def _dup_heads(w, n_heads):
    lead = w.shape[:-1]
    w = w.reshape(lead + (n_heads, 1, HEAD_DIM))
    return jnp.broadcast_to(w, lead + (n_heads, 2, HEAD_DIM)).reshape(lead + (2 * n_heads * HEAD_DIM,))


def kernel(x, norm_gains, w_qkv_a, lambda_qk_a, subln_a, w_o_a, kv_norm, w_kv_b, b_kv_b, w_q_b, b_q_b,
           sinks_b, w_o_b, w_up, conv_w, conv_b, w_down):
    B, S, D = x.shape
    T = B * S
    assert D == D_MODEL and S % ATTN_QUERY_TILE == 0 and S % ROW_TILE == 0 and S % FF_ROW_TILE == 0 and S >= 2 * WINDOW
    xt = x.reshape(T, D)
    kv3 = None
    for layer in range(DEPTH):
        g = norm_gains[layer]
        if layer < N_A_LAYERS:
            lambda_init = 0.8 - 0.6 * math.exp(-0.3 * layer)
            qkv3 = _proj(xt, g[0], w_qkv_a[layer].astype(BF16), jnp.zeros((3 * D,), F32),
                         n_scaled=DIFF_HEADS, scale=QK_SCALE * LOG2E)
            o = _attn_a(qkv3, lambda_qk_a[layer], subln_a[layer], lambda_init, B, S)
            w_o = w_o_a[layer]
        else:
            j = layer - N_A_LAYERS
            if kv3 is None:
                kvw = SWA_KV_HEADS * HEAD_DIM
                w_kv = jnp.concatenate([_dup_heads(w_kv_b[:, :kvw], SWA_KV_HEADS), w_kv_b[:, kvw:]], axis=1)
                b_kv = jnp.concatenate([_dup_heads(b_kv_b[:kvw], SWA_KV_HEADS), b_kv_b[kvw:]])
                kv3 = _proj(xt, kv_norm, w_kv.astype(BF16), b_kv)
            q3 = _proj(xt, g[0], w_q_b[j].astype(BF16), b_q_b[j],
                       n_scaled=SWA_Q_HEADS // 2, scale=QK_SCALE * LOG2E)
            o = _attn_b(q3, kv3, kv3, sinks_b[j], B, S)
            w_o = w_o_b[j]
        xt = _post_ffn(o, xt, g[1], g[2], g[3], w_o, w_up[layer], conv_w[layer], conv_b[layer],
                       w_down[layer], S, in_place=layer > 0)
    return xt.reshape(B, S, D)
```

```python
import functools
import math

import ml_dtypes
import numpy as np
import jax
import jax.numpy as jnp
from jax import lax
from jax.experimental import pallas as pl
from jax.experimental.pallas import tpu as pltpu

D_MODEL = 1024
HEAD_DIM = 64
DEPTH = 4
N_A_LAYERS = DEPTH // 2
DIFF_HEADS = D_MODEL // (2 * HEAD_DIM)
SWA_Q_HEADS = D_MODEL // HEAD_DIM
SWA_KV_HEADS = 2
SWA_GROUP = SWA_Q_HEADS // SWA_KV_HEADS
WINDOW = 128
D_FF = 2816
CONV_WIDTH = 3
NORM_EPS = 1e-6
SUBLN_EPS = 1e-5

LANES = 128
QK_SCALE = HEAD_DIM ** -0.5
VMEM_LIMIT_BYTES = 56 * 1024 * 1024
PROJ_DOT_SLABS = 4
ROW_TILE = 1024
ATTN_QUERY_TILE = 2048
ATTN_KEY_TILE = 512
ATTN_SCORES_AHEAD = 1
ATTN_SCORE_SLOTS = 2
SUM_ROWS = 16
SWA_BLOCKS_PER_STEP = 8
FF_CHUNK = 256
FF_DOT_COLS = 256
FF_ROW_TILE = 512
FF_ROW_BLOCK = 256
POS_PARTS = 3
LOG2E = math.log2(math.e)

BF16 = jnp.bfloat16
F32 = jnp.float32


def _alibi_slopes(n):
    return [2.0 ** (-8.0 * (i + 1) / n) for i in range(n)]


def _rms(x, g, eps):
    return x * lax.rsqrt(jnp.mean(x * x, axis=-1, keepdims=True) + eps) * g


def _params(n_axes):
    return pltpu.CompilerParams(
        dimension_semantics=("arbitrary",) * n_axes, vmem_limit_bytes=VMEM_LIMIT_BYTES)


def _resident(shape):
    nd = len(shape)
    return pl.BlockSpec(shape, lambda *_: (0,) * nd, pipeline_mode=pl.Buffered(1))


def _proj_kernel(x_ref, *refs, scaled):
    n = len(scaled)
    x = x_ref[...]
    xn = x * lax.rsqrt(jnp.mean(x * x, axis=-1, keepdims=True) + NORM_EPS)
    for k, (n_scaled, scale) in enumerate(scaled):
        g_ref, w_ref, b_ref = refs[3 * k:3 * k + 3]
        o_ref = refs[3 * n + k]
        n_slabs = o_ref.shape[0]
        h = (xn * g_ref[...]).astype(BF16)
        for c0 in range(0, n_slabs, PROJ_DOT_SLABS):
            nc = min(PROJ_DOT_SLABS, n_slabs - c0)
            cols = slice(c0 * LANES, (c0 + nc) * LANES)
            y = jnp.dot(h, w_ref[:, cols], preferred_element_type=F32) + b_ref[:, cols]
            for c in range(nc):
                slab = y[:, c * LANES:(c + 1) * LANES]
                if c0 + c < n_scaled:
                    slab = slab * scale
                o_ref[c0 + c] = slab.astype(BF16)


def _proj(x, groups):
    T, D = x.shape
    tm = ROW_TILE
    operands, in_specs, out_specs, out_shapes = [x], [pl.BlockSpec((tm, D), lambda i: (i, 0))], [], []
    for g, w, b, _, _ in groups:
        N = w.shape[1]
        operands += [g.reshape(1, D), w, b.reshape(1, N)]
        in_specs += [_resident((1, D)), _resident((D, N)), _resident((1, N))]
        out_specs.append(pl.BlockSpec((N // LANES, tm, LANES), lambda i: (0, i, 0)))
        out_shapes.append(jax.ShapeDtypeStruct((N // LANES, T, LANES), BF16))
    return pl.pallas_call(
        functools.partial(_proj_kernel, scaled=tuple((ns, sc) for *_, ns, sc in groups)),
        grid=(T // tm,),
        in_specs=in_specs,
        out_specs=out_specs,
        out_shape=out_shapes,
        compiler_params=_params(1),
        name="norm_proj",
    )(*operands)


def _attn_a_kernel(q_ref, k_ref, v_ref, kpos_ref, lq_ref, sg_ref, o_ref,
                   kk_scr, vt_scr, qq_scr, s_scr, m_scr, acc_scr, *, lambda_init):
    qi = pl.program_id(2)
    tq = q_ref.shape[0]
    tk = ATTN_KEY_TILE
    halves = tq // tk
    n_chunks = 2 * halves
    assert n_chunks % 2 == 0

    @pl.when(qi == 0)
    def _():
        kk_scr[:, :LANES] = k_ref[...]
        kk_scr[:, LANES:] = kpos_ref[...]
        for t in range(vt_scr.shape[0]):
            vt_scr[t, :LANES, :] = v_ref[t * tk:(t + 1) * tk, :].T
            vt_scr[t, LANES:, :] = jnp.ones((SUM_ROWS, tk), BF16)

    q = q_ref[...]
    lane = lax.broadcasted_iota(jnp.int32, (tq, LANES), 1)
    zero = jnp.zeros_like(q)
    ones_cols = jnp.where(lane < POS_PARTS, 1.0, 0.0).astype(BF16)
    qq_scr[:tq, :LANES] = jnp.where(lane < HEAD_DIM, q, zero)
    qq_scr[tq:, :LANES] = jnp.where(lane >= HEAD_DIM, q, zero)
    qq_scr[:tq, LANES:] = ones_cols
    qq_scr[tq:, LANES:] = ones_cols

    m_scr[...] = jnp.full(m_scr.shape, -jnp.inf, F32)
    acc_scr[...] = jnp.zeros(acc_scr.shape, F32)

    def scores(ki, c, slot):
        start = pl.multiple_of(ki * tk, tk)
        s_scr[slot] = lax.dot_general(
            kk_scr[pl.ds(start, tk), :], qq_scr[c * tk:(c + 1) * tk, :], (((1,), (1,)), ((), ())),
            preferred_element_type=F32)

    def softmax_pv(ki, c, slot, triangular):
        cols = slice(c * tk, (c + 1) * tk)
        s = s_scr[slot]
        if triangular:
            key = lax.broadcasted_iota(jnp.int32, (tk, tk), 0)
            qry = lax.broadcasted_iota(jnp.int32, (tk, tk), 1)
            s = jnp.where(key <= qry, s, -jnp.inf)
        m_prev = m_scr[:, cols]
        m_new = jnp.maximum(m_prev, jnp.max(s, axis=0, keepdims=True))
        alpha = jnp.exp2(m_prev - m_new)
        p = jnp.exp2(s - m_new)
        acc_scr[:, cols] = alpha * acc_scr[:, cols] + jnp.dot(
            vt_scr[ki], p.astype(BF16), preferred_element_type=F32)
        m_scr[:, cols] = m_new

    ahead = ATTN_SCORES_AHEAD
    n_slots = s_scr.shape[0]
    assert n_chunks % n_slots == 0 and ahead < n_slots and ahead <= n_chunks
    first_diag = halves * qi
    items = [(d, c) for d in range(halves) for c in range(n_chunks) if c % halves >= d]
    assert items[:ahead] == [(0, c) for c in range(ahead)]

    for c in range(ahead):
        scores(0, c, c)

    def body(ki, carry):
        for c in range(n_chunks):
            nxt = c + ahead
            scores(ki + nxt // n_chunks, nxt % n_chunks, nxt % n_slots)
            softmax_pv(ki, c, c % n_slots, False)
        return carry

    def two_tiles(kp, carry):
        return body(2 * kp + 1, body(2 * kp, carry))

    assert halves % 2 == 0
    lax.fori_loop(0, first_diag // 2, two_tiles, 0)

    for n, (d, c) in enumerate(items):
        if n + ahead < len(items):
            d_next, c_next = items[n + ahead]
            scores(first_diag + d_next, c_next, (n + ahead) % n_slots)
        softmax_pv(first_diag + d, c, n % n_slots, c % halves == d)

    lq = lq_ref[...]
    lam = (jnp.exp(jnp.sum(lq[0:1] * lq[1:2], axis=-1, keepdims=True))
           - jnp.exp(jnp.sum(lq[2:3] * lq[3:4], axis=-1, keepdims=True)) + lambda_init)
    l = acc_scr[LANES:LANES + 1, :]
    o = (acc_scr[:LANES, :tq] / l[:, :tq] - lam * (acc_scr[:LANES, tq:] / l[:, tq:]))
    o = o * lax.rsqrt(jnp.mean(o * o, axis=0, keepdims=True) + SUBLN_EPS) * sg_ref[...]
    o_ref[...] = (o * (1.0 - lambda_init)).T.astype(BF16)


def _alibi_key_columns(S):
    j = np.arange(S, dtype=np.float64)
    cols = np.zeros((DIFF_HEADS, S, LANES), np.float32)
    for h, slope in enumerate(_alibi_slopes(DIFF_HEADS)):
        rest = slope * LOG2E * j
        for part in range(POS_PARTS):
            piece = rest.astype(ml_dtypes.bfloat16).astype(np.float64)
            cols[h, :, part] = piece
            rest = rest - piece
    return jnp.asarray(cols, BF16)


def _attn_a(qkv3, lam_qk, subln_g, lambda_init, B, S):
    T = B * S
    H = DIFF_HEADS
    tq = ATTN_QUERY_TILE
    tk = ATTN_KEY_TILE
    nq = S // tq
    kernel = functools.partial(_attn_a_kernel, lambda_init=lambda_init)
    return pl.pallas_call(
        kernel,
        grid=(B, H, nq),
        in_specs=[
            pl.BlockSpec((None, tq, LANES), lambda b, h, qi: (h, b * nq + qi, 0)),
            pl.BlockSpec((None, S, LANES), lambda b, h, qi: (H + h, b, 0)),
            pl.BlockSpec((None, S, LANES), lambda b, h, qi: (2 * H + h, b, 0)),
            pl.BlockSpec((None, S, LANES), lambda b, h, qi: (h, 0, 0)),
            _resident((4, HEAD_DIM)),
            _resident((LANES, 1)),
        ],
        out_specs=pl.BlockSpec((tq, LANES), lambda b, h, qi: (b * nq + qi, h)),
        out_shape=jax.ShapeDtypeStruct((T, D_MODEL), BF16),
        scratch_shapes=[
            pltpu.VMEM((S, 2 * LANES), BF16),
            pltpu.VMEM((S // tk, LANES + SUM_ROWS, tk), BF16),
            pltpu.VMEM((2 * tq, 2 * LANES), BF16),
            pltpu.VMEM((ATTN_SCORE_SLOTS, tk, tk), F32),
            pltpu.VMEM((1, 2 * tq), F32),
            pltpu.VMEM((LANES + SUM_ROWS, 2 * tq), F32),
        ],
        compiler_params=_params(3),
        name="diff_attn",
    )(qkv3, qkv3, qkv3, _alibi_key_columns(S), lam_qk, subln_g.reshape(LANES, 1))


def _attn_b_kernel(sinks_ref, q_ref, k_ref, v_ref, bias_ref, o_ref, s_scr):
    lane = lax.broadcasted_iota(jnp.int32, (WINDOW, LANES), 1)
    pairs_per_kv = SWA_GROUP // 2
    group_lanes = SWA_GROUP * WINDOW
    ones_rows = jnp.ones((SUM_ROWS, 2 * WINDOW), BF16)

    def window_start(qb):
        nb = pl.program_id(1) * SWA_BLOCKS_PER_STEP + qb
        return nb, pl.multiple_of(jnp.maximum(nb - 1, 0) * WINDOW, WINDOW)

    def scores(qb, kh):
        _, start = window_start(qb)
        kwin = k_ref[kh, pl.ds(start, 2 * WINDOW), :]
        rows = []
        for cpair in range(pairs_per_kv):
            qc = q_ref[kh * pairs_per_kv + cpair, qb * WINDOW:(qb + 1) * WINDOW, :]
            zero = jnp.zeros_like(qc)
            rows.append(jnp.where(lane < HEAD_DIM, qc, zero))
            rows.append(jnp.where(lane >= HEAD_DIM, qc, zero))
        return lax.dot_general(kwin, jnp.concatenate(rows, axis=0), (((1,), (1,)), ((), ())),
                               preferred_element_type=F32)

    def softmax_pv(qb, kh, s_ref):
        nb, start = window_start(qb)
        bias = bias_ref.at[jnp.minimum(nb, 1) if qb == 0 else 1]
        vt = v_ref[pl.ds(start, 2 * WINDOW), :].T[kh * HEAD_DIM:(kh + 1) * HEAD_DIM]
        v_ext = jnp.concatenate([vt, ones_rows], axis=0)
        for cpair in range(pairs_per_kv):
            lanes = slice(2 * cpair * WINDOW, (2 * cpair + 2) * WINDOW)
            sink = jnp.concatenate(
                [jnp.full((1, WINDOW), sinks_ref[kh * SWA_GROUP + 2 * cpair + g] * LOG2E, F32)
                 for g in range(2)], axis=1)
            b0 = kh * group_lanes + 2 * cpair * WINDOW
            s = s_ref[:, lanes] + bias[:, b0:b0 + 2 * WINDOW]
            m = jnp.maximum(jnp.max(s, axis=0, keepdims=True), sink)
            p = jnp.exp2(s - m).astype(BF16)
            o = jnp.dot(v_ext, p, preferred_element_type=F32)
            denom = o[HEAD_DIM:HEAD_DIM + 1] + jnp.exp2(sink - m)
            o = o[:HEAD_DIM] * (1.0 / denom)
            pair = jnp.concatenate([o[:, :WINDOW], o[:, WINDOW:]], axis=0)
            col = (kh * pairs_per_kv + cpair) * LANES
            o_ref[qb * WINDOW:(qb + 1) * WINDOW, col:col + LANES] = pair.T.astype(BF16)

    items = [(qb, kh) for qb in range(SWA_BLOCKS_PER_STEP) for kh in range(SWA_KV_HEADS)]
    s_scr[0] = scores(*items[0])
    for n, item in enumerate(items):
        if n + 1 < len(items):
            s_scr[(n + 1) % 2] = scores(*items[n + 1])
        softmax_pv(*item, s_scr.at[n % 2])


def _swa_bias():
    key = np.arange(2 * WINDOW)[:, None]
    qry = np.arange(WINDOW)[None, :]
    bias = np.full((2, 2 * WINDOW, SWA_Q_HEADS * WINDOW), -np.inf, np.float32)
    for case, offset in enumerate((0, WINDOW)):
        dist = offset + qry - key
        valid = (dist >= 0) & (dist < WINDOW)
        for h, slope in enumerate(_alibi_slopes(SWA_Q_HEADS)):
            bias[case, :, h * WINDOW:(h + 1) * WINDOW] = np.where(valid, -slope * LOG2E * dist, -np.inf)
    return jnp.asarray(bias)


def _attn_b(q3, k3, v3, sinks, B, S):
    T = B * S
    rows = SWA_BLOCKS_PER_STEP * WINDOW
    nstep = S // rows
    n_pairs = SWA_Q_HEADS // 2
    return pl.pallas_call(
        _attn_b_kernel,
        grid_spec=pltpu.PrefetchScalarGridSpec(
            num_scalar_prefetch=1,
            grid=(B, nstep),
            in_specs=[
                pl.BlockSpec((n_pairs, rows, LANES), lambda b, n, *_: (0, b * nstep + n, 0)),
                pl.BlockSpec((SWA_KV_HEADS, S, LANES), lambda b, n, *_: (0, b, 0)),
                pl.BlockSpec((None, S, LANES), lambda b, n, *_: (SWA_KV_HEADS, b, 0)),
                _resident((2, 2 * WINDOW, SWA_Q_HEADS * WINDOW)),
            ],
            out_specs=pl.BlockSpec((rows, D_MODEL), lambda b, n, *_: (b * nstep + n, 0)),
            scratch_shapes=[pltpu.VMEM((2, 2 * WINDOW, SWA_GROUP * WINDOW), F32)],
        ),
        out_shape=jax.ShapeDtypeStruct((T, D_MODEL), BF16),
        compiler_params=_params(2),
        name="swa_attn",
    )(sinks, q3, k3, v3, _swa_bias())


def _gelu_tanh_of_half(h):
    c = math.sqrt(2.0 / math.pi)
    return h + h * jnp.tanh(h * (2.0 * c + (8.0 * 0.044715 * c) * (h * h)))


def _ffn_kernel(o_ref, x_ref, g1_ref, g2_ref, g3_ref, wo_ref, wgu_ref, wd_ref, cw_ref, cb_ref, out_ref,
                x1_scr, a_scr, carry_scr, gate0_scr, gate1_scr, up0_scr, up1_scr, *, tiles_per_seq):
    i = pl.program_id(0)
    tm = x_ref.shape[0]
    rb = FF_ROW_BLOCK
    n_blocks = tm // rb
    tf = FF_CHUNK
    slots = ((gate0_scr, up0_scr), (gate1_scr, up1_scr))

    @pl.when(i == 0)
    def _():
        carry_scr[...] = jnp.zeros(carry_scr.shape, F32)

    seq_start = (i % tiles_per_seq) == 0

    def up_proj(r):
        gate_scr, up_scr = slots[r % 2]
        rows = slice(r * rb, (r + 1) * rb)
        if r == 0:
            gate_scr[0:8, :] = jnp.where(seq_start, 0.0, carry_scr[...])
        else:
            gate_scr[0:8, :] = slots[(r - 1) % 2][0][rb:rb + 8, :]
        mix = jnp.dot(o_ref[rows, :], wo_ref[...], preferred_element_type=F32)
        x1 = x_ref[rows, :] + _rms(mix, g1_ref[...], NORM_EPS)
        x1_scr[rows, :] = x1
        h = _rms(x1, g2_ref[...], NORM_EPS).astype(BF16)
        for c in range(0, D_FF, FF_DOT_COLS):
            w = min(FF_DOT_COLS, D_FF - c)
            gate_scr[8:8 + rb, c:c + w] = jnp.dot(h, wgu_ref[:, c:c + w], preferred_element_type=F32)
            up_scr[:, c:c + w] = jnp.dot(h, wgu_ref[:, D_FF + c:D_FF + c + w],
                                         preferred_element_type=F32)

    def act_down(r):
        gate_scr, up_scr = slots[r % 2]
        rows = slice(r * rb, (r + 1) * rb)
        for c in range(0, D_FF, tf):
            cols = slice(c, c + tf)
            cw = 0.5 * cw_ref[:, cols]
            half_conv = (cw[2:3] * gate_scr[8:8 + rb, cols] + 0.5 * cb_ref[:, cols]
                         + cw[1:2] * gate_scr[7:7 + rb, cols]
                         + cw[0:1] * gate_scr[6:6 + rb, cols])
            a_scr[:, cols] = (_gelu_tanh_of_half(half_conv) * up_scr[:, cols]).astype(BF16)
        f = jnp.dot(a_scr[...], wd_ref[...], preferred_element_type=F32)
        out_ref[rows, :] = x1_scr[rows, :] + _rms(f, g3_ref[...], NORM_EPS)
        if r == n_blocks - 1:
            carry_scr[...] = gate_scr[rb:rb + 8, :]

    up_proj(0)
    for r in range(n_blocks):
        if r + 1 < n_blocks:
            up_proj(r + 1)
        act_down(r)


def _post_ffn(o, x, g1, g2, g3, w_o, w_up, conv_w, conv_b, w_down, S, in_place):
    T, D = x.shape
    tm = FF_ROW_TILE
    rb = FF_ROW_BLOCK
    kernel = functools.partial(_ffn_kernel, tiles_per_seq=S // tm)
    return pl.pallas_call(
        kernel,
        grid=(T // tm,),
        in_specs=[
            pl.BlockSpec((tm, D), lambda i: (i, 0)),
            pl.BlockSpec((tm, D), lambda i: (i, 0)),
            _resident((1, D)),
            _resident((1, D)),
            _resident((1, D)),
            _resident((D, D)),
            _resident((D, 2 * D_FF)),
            _resident((D_FF, D)),
            _resident((CONV_WIDTH, D_FF)),
            _resident((1, D_FF)),
        ],
        out_specs=pl.BlockSpec((tm, D), lambda i: (i, 0)),
        out_shape=jax.ShapeDtypeStruct((T, D), F32),
        scratch_shapes=[
            pltpu.VMEM((tm, D), F32),
            pltpu.VMEM((rb, D_FF), BF16),
            pltpu.VMEM((8, D_FF), F32),
            pltpu.VMEM((rb + 8, D_FF), F32),
            pltpu.VMEM((rb + 8, D_FF), F32),
            pltpu.VMEM((rb, D_FF), F32),
            pltpu.VMEM((rb, D_FF), F32),
        ],
        input_output_aliases={1: 0} if in_place else {},
        compiler_params=_params(1),
        name="post_ffn",
    )(o, x, g1.reshape(1, D), g2.reshape(1, D), g3.reshape(1, D), w_o.astype(BF16),
      w_up.astype(BF16), w_down.astype(BF16), conv_w, conv_b.reshape(1, D_FF))


def _dup_heads(w, n_heads):
    lead = w.shape[:-1]
    w = w.reshape(lead + (n_heads, 1, HEAD_DIM))
    return jnp.broadcast_to(w, lead + (n_heads, 2, HEAD_DIM)).reshape(lead + (2 * n_heads * HEAD_DIM,))


def kernel(x, norm_gains, w_qkv_a, lambda_qk_a, subln_a, w_o_a, kv_norm, w_kv_b, b_kv_b, w_q_b, b_q_b,
           sinks_b, w_o_b, w_up, conv_w, conv_b, w_down):
    B, S, D = x.shape
    T = B * S
    assert D == D_MODEL and S % ATTN_QUERY_TILE == 0 and S % ROW_TILE == 0 and S % FF_ROW_TILE == 0 and S >= 2 * WINDOW
    xt = x.reshape(T, D)
    kv3 = None
    for layer in range(DEPTH):
        g = norm_gains[layer]
        if layer < N_A_LAYERS:
            lambda_init = 0.8 - 0.6 * math.exp(-0.3 * layer)
            qkv3, = _proj(xt, [(g[0], w_qkv_a[layer].astype(BF16), jnp.zeros((3 * D,), F32),
                                DIFF_HEADS, QK_SCALE * LOG2E)])
            o = _attn_a(qkv3, lambda_qk_a[layer], subln_a[layer], lambda_init, B, S)
            w_o = w_o_a[layer]
        else:
            j = layer - N_A_LAYERS
            q_group = (g[0], w_q_b[j].astype(BF16), b_q_b[j], SWA_Q_HEADS // 2, QK_SCALE * LOG2E)
            if kv3 is None:
                kvw = SWA_KV_HEADS * HEAD_DIM
                w_kv = jnp.concatenate([_dup_heads(w_kv_b[:, :kvw], SWA_KV_HEADS), w_kv_b[:, kvw:]], axis=1)
                b_kv = jnp.concatenate([_dup_heads(b_kv_b[:kvw], SWA_KV_HEADS), b_kv_b[kvw:]])
                q3, kv3 = _proj(xt, [q_group, (kv_norm, w_kv.astype(BF16), b_kv, 0, 1.0)])
            else:
                q3, = _proj(xt, [q_group])
            o = _attn_b(q3, kv3, kv3, sinks_b[j], B, S)
            w_o = w_o_b[j]
        xt = _post_ffn(o, xt, g[1], g[2], g[3], w_o, w_up[layer], conv_w[layer], conv_b[layer],
                       w_down[layer], S, in_place=layer > 0)
    return xt.reshape(B, S, D)
```

```python
import functools
import math

import ml_dtypes
import numpy as np
import jax
import jax.numpy as jnp
from jax import lax
from jax.experimental import pallas as pl
from jax.experimental.pallas import tpu as pltpu

D_MODEL = 1024
HEAD_DIM = 64
DEPTH = 4
N_A_LAYERS = DEPTH // 2
DIFF_HEADS = D_MODEL // (2 * HEAD_DIM)
SWA_Q_HEADS = D_MODEL // HEAD_DIM
SWA_KV_HEADS = 2
SWA_GROUP = SWA_Q_HEADS // SWA_KV_HEADS
WINDOW = 128
D_FF = 2816
CONV_WIDTH = 3
NORM_EPS = 1e-6
SUBLN_EPS = 1e-5

LANES = 128
QK_SCALE = HEAD_DIM ** -0.5
VMEM_LIMIT_BYTES = 56 * 1024 * 1024
PROJ_DOT_SLABS = 4
ROW_TILE = 1024
ATTN_QUERY_TILE = 2048
ATTN_KEY_TILE = 512
ATTN_SCORES_AHEAD = 1
ATTN_SCORE_SLOTS = 2
SUM_ROWS = 16
SWA_BLOCKS_PER_STEP = 8
FF_CHUNK = 256
FF_DOT_COLS = 256
FF_ROW_TILE = 512
FF_ROW_BLOCK = 256
POS_PARTS = 3
LOG2E = math.log2(math.e)

BF16 = jnp.bfloat16
F32 = jnp.float32


def _alibi_slopes(n):
    return [2.0 ** (-8.0 * (i + 1) / n) for i in range(n)]


def _rms(x, g, eps):
    return x * lax.rsqrt(jnp.mean(x * x, axis=-1, keepdims=True) + eps) * g


def _params(n_axes):
    return pltpu.CompilerParams(
        dimension_semantics=("arbitrary",) * n_axes, vmem_limit_bytes=VMEM_LIMIT_BYTES)


def _resident(shape):
    nd = len(shape)
    return pl.BlockSpec(shape, lambda *_: (0,) * nd, pipeline_mode=pl.Buffered(1))


def _proj_kernel(x_ref, *refs, scaled):
    n = len(scaled)
    x = x_ref[...]
    xn = x * lax.rsqrt(jnp.mean(x * x, axis=-1, keepdims=True) + NORM_EPS)
    for k, (n_scaled, scale) in enumerate(scaled):
        g_ref, w_ref, b_ref = refs[3 * k:3 * k + 3]
        o_ref = refs[3 * n + k]
        n_slabs = o_ref.shape[0]
        h = (xn * g_ref[...]).astype(BF16)
        for c0 in range(0, n_slabs, PROJ_DOT_SLABS):
            nc = min(PROJ_DOT_SLABS, n_slabs - c0)
            cols = slice(c0 * LANES, (c0 + nc) * LANES)
            y = jnp.dot(h, w_ref[:, cols], preferred_element_type=F32) + b_ref[:, cols]
            for c in range(nc):
                slab = y[:, c * LANES:(c + 1) * LANES]
                if c0 + c < n_scaled:
                    slab = slab * scale
                o_ref[c0 + c] = slab.astype(BF16)


def _proj(x, groups):
    T, D = x.shape
    tm = ROW_TILE
    operands, in_specs, out_specs, out_shapes = [x], [pl.BlockSpec((tm, D), lambda i: (i, 0))], [], []
    for g, w, b, _, _ in groups:
        N = w.shape[1]
        operands += [g.reshape(1, D), w, b.reshape(1, N)]
        in_specs += [_resident((1, D)), _resident((D, N)), _resident((1, N))]
        out_specs.append(pl.BlockSpec((N // LANES, tm, LANES), lambda i: (0, i, 0)))
        out_shapes.append(jax.ShapeDtypeStruct((N // LANES, T, LANES), BF16))
    return pl.pallas_call(
        functools.partial(_proj_kernel, scaled=tuple((ns, sc) for *_, ns, sc in groups)),
        grid=(T // tm,),
        in_specs=in_specs,
        out_specs=out_specs,
        out_shape=out_shapes,
        compiler_params=_params(1),
        name="norm_proj",
    )(*operands)


def _attn_a_kernel(q_ref, k_ref, v_ref, kpos_ref, lq_ref, sg_ref, o_ref,
                   kk_scr, vt_scr, qq_scr, s_scr, m_scr, acc_scr, *, lambda_init):
    qi = pl.program_id(2)
    tq = q_ref.shape[0]
    tk = ATTN_KEY_TILE
    halves = tq // tk
    n_chunks = 2 * halves
    assert n_chunks % 2 == 0

    @pl.when(qi == 0)
    def _():
        kk_scr[:, :LANES] = k_ref[...]
        kk_scr[:, LANES:] = kpos_ref[...]
        for t in range(vt_scr.shape[0]):
            vt_scr[t, :LANES, :] = v_ref[t * tk:(t + 1) * tk, :].T
            vt_scr[t, LANES:, :] = jnp.ones((SUM_ROWS, tk), BF16)

    lane = lax.broadcasted_iota(jnp.int32, (tk, LANES), 1)
    for c in range(n_chunks):
        q = q_ref[(c % halves) * tk:(c % halves + 1) * tk, :]
        keep = (lane < HEAD_DIM) if c < halves else (lane >= HEAD_DIM)
        qq_scr[:LANES, c * tk:(c + 1) * tk] = jnp.where(keep, q, jnp.zeros_like(q)).T
    row = lax.broadcasted_iota(jnp.int32, (LANES, 2 * tq), 0)
    qq_scr[LANES:, :] = jnp.where(row < POS_PARTS, 1.0, 0.0).astype(BF16)

    m_scr[...] = jnp.full(m_scr.shape, -jnp.inf, F32)
    acc_scr[...] = jnp.zeros(acc_scr.shape, F32)

    def scores(ki, c, slot):
        start = pl.multiple_of(ki * tk, tk)
        s_scr[slot] = jnp.dot(kk_scr[pl.ds(start, tk), :], qq_scr[:, c * tk:(c + 1) * tk],
                              preferred_element_type=F32)

    def softmax_pv(ki, c, slot, triangular):
        cols = slice(c * tk, (c + 1) * tk)
        s = s_scr[slot]
        if triangular:
            key = lax.broadcasted_iota(jnp.int32, (tk, tk), 0)
            qry = lax.broadcasted_iota(jnp.int32, (tk, tk), 1)
            s = jnp.where(key <= qry, s, -jnp.inf)
        m_prev = m_scr[:, cols]
        m_new = jnp.maximum(m_prev, jnp.max(s, axis=0, keepdims=True))
        alpha = jnp.exp2(m_prev - m_new)
        p = jnp.exp2(s - m_new)
        acc_scr[:, cols] = alpha * acc_scr[:, cols] + jnp.dot(
            vt_scr[ki], p.astype(BF16), preferred_element_type=F32)
        m_scr[:, cols] = m_new

    ahead = ATTN_SCORES_AHEAD
    n_slots = s_scr.shape[0]
    assert n_chunks % n_slots == 0 and ahead < n_slots and ahead <= n_chunks
    first_diag = halves * qi
    items = [(d, c) for d in range(halves) for c in range(n_chunks) if c % halves >= d]
    assert items[:ahead] == [(0, c) for c in range(ahead)]

    for c in range(ahead):
        scores(0, c, c)

    def body(ki, carry):
        for c in range(n_chunks):
            nxt = c + ahead
            scores(ki + nxt // n_chunks, nxt % n_chunks, nxt % n_slots)
            softmax_pv(ki, c, c % n_slots, False)
        return carry

    def two_tiles(kp, carry):
        return body(2 * kp + 1, body(2 * kp, carry))

    assert halves % 2 == 0
    lax.fori_loop(0, first_diag // 2, two_tiles, 0)

    for n, (d, c) in enumerate(items):
        if n + ahead < len(items):
            d_next, c_next = items[n + ahead]
            scores(first_diag + d_next, c_next, (n + ahead) % n_slots)
        softmax_pv(first_diag + d, c, n % n_slots, c % halves == d)

    lq = lq_ref[...]
    lam = (jnp.exp(jnp.sum(lq[0:1] * lq[1:2], axis=-1, keepdims=True))
           - jnp.exp(jnp.sum(lq[2:3] * lq[3:4], axis=-1, keepdims=True)) + lambda_init)
    l = acc_scr[LANES:LANES + 1, :]
    o = (acc_scr[:LANES, :tq] / l[:, :tq] - lam * (acc_scr[:LANES, tq:] / l[:, tq:]))
    o = o * lax.rsqrt(jnp.mean(o * o, axis=0, keepdims=True) + SUBLN_EPS) * sg_ref[...]
    o_ref[...] = (o * (1.0 - lambda_init)).T.astype(BF16)


def _alibi_key_columns(S):
    j = np.arange(S, dtype=np.float64)
    cols = np.zeros((DIFF_HEADS, S, LANES), np.float32)
    for h, slope in enumerate(_alibi_slopes(DIFF_HEADS)):
        rest = slope * LOG2E * j
        for part in range(POS_PARTS):
            piece = rest.astype(ml_dtypes.bfloat16).astype(np.float64)
            cols[h, :, part] = piece
            rest = rest - piece
    return jnp.asarray(cols, BF16)


def _attn_a(qkv3, lam_qk, subln_g, lambda_init, B, S):
    T = B * S
    H = DIFF_HEADS
    tq = ATTN_QUERY_TILE
    tk = ATTN_KEY_TILE
    nq = S // tq
    kernel = functools.partial(_attn_a_kernel, lambda_init=lambda_init)
    return pl.pallas_call(
        kernel,
        grid=(B, H, nq),
        in_specs=[
            pl.BlockSpec((None, tq, LANES), lambda b, h, qi: (h, b * nq + qi, 0)),
            pl.BlockSpec((None, S, LANES), lambda b, h, qi: (H + h, b, 0)),
            pl.BlockSpec((None, S, LANES), lambda b, h, qi: (2 * H + h, b, 0)),
            pl.BlockSpec((None, S, LANES), lambda b, h, qi: (h, 0, 0)),
            _resident((4, HEAD_DIM)),
            _resident((LANES, 1)),
        ],
        out_specs=pl.BlockSpec((tq, LANES), lambda b, h, qi: (b * nq + qi, h)),
        out_shape=jax.ShapeDtypeStruct((T, D_MODEL), BF16),
        scratch_shapes=[
            pltpu.VMEM((S, 2 * LANES), BF16),
            pltpu.VMEM((S // tk, LANES + SUM_ROWS, tk), BF16),
            pltpu.VMEM((2 * LANES, 2 * tq), BF16),
            pltpu.VMEM((ATTN_SCORE_SLOTS, tk, tk), F32),
            pltpu.VMEM((1, 2 * tq), F32),
            pltpu.VMEM((LANES + SUM_ROWS, 2 * tq), F32),
        ],
        compiler_params=_params(3),
        name="diff_attn",
    )(qkv3, qkv3, qkv3, _alibi_key_columns(S), lam_qk, subln_g.reshape(LANES, 1))


def _attn_b_kernel(sinks_ref, q_ref, k_ref, v_ref, bias_ref, o_ref, s_scr):
    lane = lax.broadcasted_iota(jnp.int32, (WINDOW, LANES), 1)
    pairs_per_kv = SWA_GROUP // 2
    group_lanes = SWA_GROUP * WINDOW
    ones_rows = jnp.ones((SUM_ROWS, 2 * WINDOW), BF16)

    def window_start(qb):
        nb = pl.program_id(1) * SWA_BLOCKS_PER_STEP + qb
        return nb, pl.multiple_of(jnp.maximum(nb - 1, 0) * WINDOW, WINDOW)

    def scores(qb, kh):
        _, start = window_start(qb)
        kwin = k_ref[kh, pl.ds(start, 2 * WINDOW), :]
        rows = []
        for cpair in range(pairs_per_kv):
            qc = q_ref[kh * pairs_per_kv + cpair, qb * WINDOW:(qb + 1) * WINDOW, :]
            zero = jnp.zeros_like(qc)
            rows.append(jnp.where(lane < HEAD_DIM, qc, zero))
            rows.append(jnp.where(lane >= HEAD_DIM, qc, zero))
        return lax.dot_general(kwin, jnp.concatenate(rows, axis=0), (((1,), (1,)), ((), ())),
                               preferred_element_type=F32)

    def softmax_pv(qb, kh, s_ref):
        nb, start = window_start(qb)
        bias = bias_ref.at[jnp.minimum(nb, 1) if qb == 0 else 1]
        vt = v_ref[pl.ds(start, 2 * WINDOW), :].T[kh * HEAD_DIM:(kh + 1) * HEAD_DIM]
        v_ext = jnp.concatenate([vt, ones_rows], axis=0)
        for cpair in range(pairs_per_kv):
            lanes = slice(2 * cpair * WINDOW, (2 * cpair + 2) * WINDOW)
            sink = jnp.concatenate(
                [jnp.full((1, WINDOW), sinks_ref[kh * SWA_GROUP + 2 * cpair + g] * LOG2E, F32)
                 for g in range(2)], axis=1)
            b0 = kh * group_lanes + 2 * cpair * WINDOW
            s = s_ref[:, lanes] + bias[:, b0:b0 + 2 * WINDOW]
            m = jnp.maximum(jnp.max(s, axis=0, keepdims=True), sink)
            p = jnp.exp2(s - m).astype(BF16)
            o = jnp.dot(v_ext, p, preferred_element_type=F32)
            denom = o[HEAD_DIM:HEAD_DIM + 1] + jnp.exp2(sink - m)
            o = o[:HEAD_DIM] * (1.0 / denom)
            pair = jnp.concatenate([o[:, :WINDOW], o[:, WINDOW:]], axis=0)
            col = (kh * pairs_per_kv + cpair) * LANES
            o_ref[qb * WINDOW:(qb + 1) * WINDOW, col:col + LANES] = pair.T.astype(BF16)

    items = [(qb, kh) for qb in range(SWA_BLOCKS_PER_STEP) for kh in range(SWA_KV_HEADS)]
    s_scr[0] = scores(*items[0])
    for n, item in enumerate(items):
        if n + 1 < len(items):
            s_scr[(n + 1) % 2] = scores(*items[n + 1])
        softmax_pv(*item, s_scr.at[n % 2])


def _swa_bias():
    key = np.arange(2 * WINDOW)[:, None]
    qry = np.arange(WINDOW)[None, :]
    bias = np.full((2, 2 * WINDOW, SWA_Q_HEADS * WINDOW), -np.inf, np.float32)
    for case, offset in enumerate((0, WINDOW)):
        dist = offset + qry - key
        valid = (dist >= 0) & (dist < WINDOW)
        for h, slope in enumerate(_alibi_slopes(SWA_Q_HEADS)):
            bias[case, :, h * WINDOW:(h + 1) * WINDOW] = np.where(valid, -slope * LOG2E * dist, -np.inf)
    return jnp.asarray(bias)


def _attn_b(q3, k3, v3, sinks, B, S):
    T = B * S
    rows = SWA_BLOCKS_PER_STEP * WINDOW
    nstep = S // rows
    n_pairs = SWA_Q_HEADS // 2
    return pl.pallas_call(
        _attn_b_kernel,
        grid_spec=pltpu.PrefetchScalarGridSpec(
            num_scalar_prefetch=1,
            grid=(B, nstep),
            in_specs=[
                pl.BlockSpec((n_pairs, rows, LANES), lambda b, n, *_: (0, b * nstep + n, 0)),
                pl.BlockSpec((SWA_KV_HEADS, S, LANES), lambda b, n, *_: (0, b, 0)),
                pl.BlockSpec((None, S, LANES), lambda b, n, *_: (SWA_KV_HEADS, b, 0)),
                _resident((2, 2 * WINDOW, SWA_Q_HEADS * WINDOW)),
            ],
            out_specs=pl.BlockSpec((rows, D_MODEL), lambda b, n, *_: (b * nstep + n, 0)),
            scratch_shapes=[pltpu.VMEM((2, 2 * WINDOW, SWA_GROUP * WINDOW), F32)],
        ),
        out_shape=jax.ShapeDtypeStruct((T, D_MODEL), BF16),
        compiler_params=_params(2),
        name="swa_attn",
    )(sinks, q3, k3, v3, _swa_bias())


def _gelu_tanh_of_half(h):
    c = math.sqrt(2.0 / math.pi)
    return h + h * jnp.tanh(h * (2.0 * c + (8.0 * 0.044715 * c) * (h * h)))


def _ffn_kernel(o_ref, x_ref, g1_ref, g2_ref, g3_ref, wo_ref, wgu_ref, wd_ref, cw_ref, cb_ref, out_ref,
                x1_scr, a_scr, carry_scr, gate0_scr, gate1_scr, up0_scr, up1_scr, *, tiles_per_seq):
    i = pl.program_id(0)
    tm = x_ref.shape[0]
    rb = FF_ROW_BLOCK
    n_blocks = tm // rb
    tf = FF_CHUNK
    slots = ((gate0_scr, up0_scr), (gate1_scr, up1_scr))

    @pl.when(i == 0)
    def _():
        carry_scr[...] = jnp.zeros(carry_scr.shape, F32)

    seq_start = (i % tiles_per_seq) == 0

    def up_proj(r):
        gate_scr, up_scr = slots[r % 2]
        rows = slice(r * rb, (r + 1) * rb)
        if r == 0:
            gate_scr[0:8, :] = jnp.where(seq_start, 0.0, carry_scr[...])
        else:
            gate_scr[0:8, :] = slots[(r - 1) % 2][0][rb:rb + 8, :]
        mix = jnp.dot(o_ref[rows, :], wo_ref[...], preferred_element_type=F32)
        x1 = x_ref[rows, :] + _rms(mix, g1_ref[...], NORM_EPS)
        x1_scr[rows, :] = x1
        h = _rms(x1, g2_ref[...], NORM_EPS).astype(BF16)
        for c in range(0, D_FF, FF_DOT_COLS):
            w = min(FF_DOT_COLS, D_FF - c)
            gate_scr[8:8 + rb, c:c + w] = jnp.dot(h, wgu_ref[:, c:c + w], preferred_element_type=F32)
            up_scr[:, c:c + w] = jnp.dot(h, wgu_ref[:, D_FF + c:D_FF + c + w],
                                         preferred_element_type=F32)

    def act_down(r):
        gate_scr, up_scr = slots[r % 2]
        rows = slice(r * rb, (r + 1) * rb)
        for c in range(0, D_FF, tf):
            cols = slice(c, c + tf)
            cw = 0.5 * cw_ref[:, cols]
            half_conv = (cw[2:3] * gate_scr[8:8 + rb, cols] + 0.5 * cb_ref[:, cols]
                         + cw[1:2] * gate_scr[7:7 + rb, cols]
                         + cw[0:1] * gate_scr[6:6 + rb, cols])
            a_scr[:, cols] = (_gelu_tanh_of_half(half_conv) * up_scr[:, cols]).astype(BF16)
        f = jnp.dot(a_scr[...], wd_ref[...], preferred_element_type=F32)
        out_ref[rows, :] = x1_scr[rows, :] + _rms(f, g3_ref[...], NORM_EPS)
        if r == n_blocks - 1:
            carry_scr[...] = gate_scr[rb:rb + 8, :]

    up_proj(0)
    for r in range(n_blocks):
        if r + 1 < n_blocks:
            up_proj(r + 1)
        act_down(r)


def _post_ffn(o, x, g1, g2, g3, w_o, w_up, conv_w, conv_b, w_down, S, in_place):
    T, D = x.shape
    tm = FF_ROW_TILE
    rb = FF_ROW_BLOCK
    kernel = functools.partial(_ffn_kernel, tiles_per_seq=S // tm)
    return pl.pallas_call(
        kernel,
        grid=(T // tm,),
        in_specs=[
            pl.BlockSpec((tm, D), lambda i: (i, 0)),
            pl.BlockSpec((tm, D), lambda i: (i, 0)),
            _resident((1, D)),
            _resident((1, D)),
            _resident((1, D)),
            _resident((D, D)),
            _resident((D, 2 * D_FF)),
            _resident((D_FF, D)),
            _resident((CONV_WIDTH, D_FF)),
            _resident((1, D_FF)),
        ],
        out_specs=pl.BlockSpec((tm, D), lambda i: (i, 0)),
        out_shape=jax.ShapeDtypeStruct((T, D), F32),
        scratch_shapes=[
            pltpu.VMEM((tm, D), F32),
            pltpu.VMEM((rb, D_FF), BF16),
            pltpu.VMEM((8, D_FF), F32),
            pltpu.VMEM((rb + 8, D_FF), F32),
            pltpu.VMEM((rb + 8, D_FF), F32),
            pltpu.VMEM((rb, D_FF), F32),
            pltpu.VMEM((rb, D_FF), F32),
        ],
        input_output_aliases={1: 0} if in_place else {},
        compiler_params=_params(1),
        name="post_ffn",
    )(o, x, g1.reshape(1, D), g2.reshape(1, D), g3.reshape(1, D), w_o.astype(BF16),
      w_up.astype(BF16), w_down.astype(BF16), conv_w, conv_b.reshape(1, D_FF))


def _dup_heads(w, n_heads):
    lead = w.shape[:-1]
    w = w.reshape(lead + (n_heads, 1, HEAD_DIM))
    return jnp.broadcast_to(w, lead + (n_heads, 2, HEAD_DIM)).reshape(lead + (2 * n_heads * HEAD_DIM,))


def kernel(x, norm_gains, w_qkv_a, lambda_qk_a, subln_a, w_o_a, kv_norm, w_kv_b, b_kv_b, w_q_b, b_q_b,
           sinks_b, w_o_b, w_up, conv_w, conv_b, w_down):
    B, S, D = x.shape
    T = B * S
    assert D == D_MODEL and S % ATTN_QUERY_TILE == 0 and S % ROW_TILE == 0 and S % FF_ROW_TILE == 0 and S >= 2 * WINDOW
    xt = x.reshape(T, D)
    kv3 = None
    for layer in range(DEPTH):
        g = norm_gains[layer]
        if layer < N_A_LAYERS:
            lambda_init = 0.8 - 0.6 * math.exp(-0.3 * layer)
            qkv3, = _proj(xt, [(g[0], w_qkv_a[layer].astype(BF16), jnp.zeros((3 * D,), F32),
                                DIFF_HEADS, QK_SCALE * LOG2E)])
            o = _attn_a(qkv3, lambda_qk_a[layer], subln_a[layer], lambda_init, B, S)
            w_o = w_o_a[layer]
        else:
            j = layer - N_A_LAYERS
            q_group = (g[0], w_q_b[j].astype(BF16), b_q_b[j], SWA_Q_HEADS // 2, QK_SCALE * LOG2E)
            if kv3 is None:
                kvw = SWA_KV_HEADS * HEAD_DIM
                w_kv = jnp.concatenate([_dup_heads(w_kv_b[:, :kvw], SWA_KV_HEADS), w_kv_b[:, kvw:]], axis=1)
                b_kv = jnp.concatenate([_dup_heads(b_kv_b[:kvw], SWA_KV_HEADS), b_kv_b[kvw:]])
                q3, kv3 = _proj(xt, [q_group, (kv_norm, w_kv.astype(BF16), b_kv, 0, 1.0)])
            else:
                q3, = _proj(xt, [q_group])
            o = _attn_b(q3, kv3, kv3, sinks_b[j], B, S)
            w_o = w_o_b[j]
        xt = _post_ffn(o, xt, g[1], g[2], g[3], w_o, w_up[layer], conv_w[layer], conv_b[layer],
                       w_down[layer], S, in_place=layer > 0)
    return xt.reshape(B, S, D)
```

```python
import functools
import math

import ml_dtypes
import numpy as np
import jax
import jax.numpy as jnp
from jax import lax
from jax.experimental import pallas as pl
from jax.experimental.pallas import tpu as pltpu

D_MODEL = 1024
HEAD_DIM = 64
DEPTH = 4
N_A_LAYERS = DEPTH // 2
DIFF_HEADS = D_MODEL // (2 * HEAD_DIM)
SWA_Q_HEADS = D_MODEL // HEAD_DIM
SWA_KV_HEADS = 2
SWA_GROUP = SWA_Q_HEADS // SWA_KV_HEADS
WINDOW = 128
D_FF = 2816
CONV_WIDTH = 3
NORM_EPS = 1e-6
SUBLN_EPS = 1e-5

LANES = 128
QK_SCALE = HEAD_DIM ** -0.5
VMEM_LIMIT_BYTES = 56 * 1024 * 1024
PROJ_DOT_SLABS = 4
ROW_TILE = 1024
ATTN_QUERY_TILE = 2048
ATTN_KEY_TILE = 512
ATTN_SCORES_AHEAD = 1
ATTN_SCORE_SLOTS = 2
SUM_ROWS = 16
SWA_BLOCKS_PER_STEP = 8
FF_CHUNK = 256
FF_DOT_COLS = 256
FF_ROW_TILE = 512
FF_ROW_BLOCK = 256
POS_PARTS = 3
LOG2E = math.log2(math.e)

BF16 = jnp.bfloat16
F32 = jnp.float32


def _alibi_slopes(n):
    return [2.0 ** (-8.0 * (i + 1) / n) for i in range(n)]


def _rms(x, g, eps):
    return x * lax.rsqrt(jnp.mean(x * x, axis=-1, keepdims=True) + eps) * g


def _params(n_axes):
    return pltpu.CompilerParams(
        dimension_semantics=("arbitrary",) * n_axes, vmem_limit_bytes=VMEM_LIMIT_BYTES)


def _resident(shape):
    nd = len(shape)
    return pl.BlockSpec(shape, lambda *_: (0,) * nd, pipeline_mode=pl.Buffered(1))


def _proj_kernel(x_ref, *refs, scaled):
    n = len(scaled)
    x = x_ref[...]
    xn = x * lax.rsqrt(jnp.mean(x * x, axis=-1, keepdims=True) + NORM_EPS)
    for k, (n_scaled, scale) in enumerate(scaled):
        g_ref, w_ref, b_ref = refs[3 * k:3 * k + 3]
        o_ref = refs[3 * n + k]
        n_slabs = o_ref.shape[0]
        h = (xn * g_ref[...]).astype(BF16)
        for c0 in range(0, n_slabs, PROJ_DOT_SLABS):
            nc = min(PROJ_DOT_SLABS, n_slabs - c0)
            cols = slice(c0 * LANES, (c0 + nc) * LANES)
            y = jnp.dot(h, w_ref[:, cols], preferred_element_type=F32) + b_ref[:, cols]
            for c in range(nc):
                slab = y[:, c * LANES:(c + 1) * LANES]
                if c0 + c < n_scaled:
                    slab = slab * scale
                o_ref[c0 + c] = slab.astype(BF16)


def _proj(x, groups):
    T, D = x.shape
    tm = ROW_TILE
    operands, in_specs, out_specs, out_shapes = [x], [pl.BlockSpec((tm, D), lambda i: (i, 0))], [], []
    for g, w, b, _, _ in groups:
        N = w.shape[1]
        operands += [g.reshape(1, D), w, b.reshape(1, N)]
        in_specs += [_resident((1, D)), _resident((D, N)), _resident((1, N))]
        out_specs.append(pl.BlockSpec((N // LANES, tm, LANES), lambda i: (0, i, 0)))
        out_shapes.append(jax.ShapeDtypeStruct((N // LANES, T, LANES), BF16))
    return pl.pallas_call(
        functools.partial(_proj_kernel, scaled=tuple((ns, sc) for *_, ns, sc in groups)),
        grid=(T // tm,),
        in_specs=in_specs,
        out_specs=out_specs,
        out_shape=out_shapes,
        compiler_params=_params(1),
        name="norm_proj",
    )(*operands)


def _attn_a_kernel(q_ref, k_ref, v_ref, kpos_ref, lq_ref, sg_ref, o_ref,
                   kk_scr, vt_scr, qq_scr, s_scr, m_scr, acc_scr, *, lambda_init):
    qi = pl.program_id(2)
    tq = q_ref.shape[0]
    tk = ATTN_KEY_TILE
    halves = tq // tk
    n_chunks = 2 * halves
    assert n_chunks % 2 == 0

    @pl.when(qi == 0)
    def _():
        kk_scr[:, :LANES] = k_ref[...]
        kk_scr[:, LANES:] = kpos_ref[...]
        for t in range(vt_scr.shape[0]):
            vt_scr[t, :LANES, :] = v_ref[t * tk:(t + 1) * tk, :].T
            vt_scr[t, LANES:, :] = jnp.ones((SUM_ROWS, tk), BF16)

    lane = lax.broadcasted_iota(jnp.int32, (tk, LANES), 1)
    for c in range(n_chunks):
        q = q_ref[(c % halves) * tk:(c % halves + 1) * tk, :]
        keep = (lane < HEAD_DIM) if c < halves else (lane >= HEAD_DIM)
        qq_scr[:LANES, c * tk:(c + 1) * tk] = jnp.where(keep, q, jnp.zeros_like(q)).T
    row = lax.broadcasted_iota(jnp.int32, (LANES, 2 * tq), 0)
    qq_scr[LANES:, :] = jnp.where(row < POS_PARTS, 1.0, 0.0).astype(BF16)

    m_scr[...] = jnp.full(m_scr.shape, -jnp.inf, F32)
    acc_scr[...] = jnp.zeros(acc_scr.shape, F32)

    def scores(ki, c, slot):
        start = pl.multiple_of(ki * tk, tk)
        s_scr[slot] = jnp.dot(kk_scr[pl.ds(start, tk), :], qq_scr[:, c * tk:(c + 1) * tk],
                              preferred_element_type=F32)

    def softmax_pv(ki, c, slot, triangular):
        cols = slice(c * tk, (c + 1) * tk)
        s = s_scr[slot]
        if triangular:
            key = lax.broadcasted_iota(jnp.int32, (tk, tk), 0)
            qry = lax.broadcasted_iota(jnp.int32, (tk, tk), 1)
            s = jnp.where(key <= qry, s, -jnp.inf)
        m_prev = m_scr[:, cols]
        m_new = jnp.maximum(m_prev, jnp.max(s, axis=0, keepdims=True))
        alpha = jnp.exp2(m_prev - m_new)
        p = jnp.exp2(s - m_new)
        acc_scr[:, cols] = alpha * acc_scr[:, cols] + jnp.dot(
            vt_scr[ki], p.astype(BF16), preferred_element_type=F32)
        m_scr[:, cols] = m_new

    ahead = ATTN_SCORES_AHEAD
    n_slots = s_scr.shape[0]
    assert n_chunks % n_slots == 0 and ahead < n_slots and ahead <= n_chunks
    first_diag = halves * qi
    items = [(d, c) for d in range(halves) for c in range(n_chunks) if c % halves >= d]
    assert items[:ahead] == [(0, c) for c in range(ahead)]

    for c in range(ahead):
        scores(0, c, c)

    def body(ki, carry):
        for c in range(n_chunks):
            nxt = c + ahead
            scores(ki + nxt // n_chunks, nxt % n_chunks, nxt % n_slots)
            softmax_pv(ki, c, c % n_slots, False)
        return carry

    def two_tiles(kp, carry):
        return body(2 * kp + 1, body(2 * kp, carry))

    assert halves % 2 == 0
    lax.fori_loop(0, first_diag // 2, two_tiles, 0)

    for n, (d, c) in enumerate(items):
        if n + ahead < len(items):
            d_next, c_next = items[n + ahead]
            scores(first_diag + d_next, c_next, (n + ahead) % n_slots)
        softmax_pv(first_diag + d, c, n % n_slots, c % halves == d)

    lq = lq_ref[...]
    lam = (jnp.exp(jnp.sum(lq[0:1] * lq[1:2], axis=-1, keepdims=True))
           - jnp.exp(jnp.sum(lq[2:3] * lq[3:4], axis=-1, keepdims=True)) + lambda_init)
    l = acc_scr[LANES:LANES + 1, :]
    o = (acc_scr[:LANES, :tq] / l[:, :tq] - lam * (acc_scr[:LANES, tq:] / l[:, tq:]))
    o = o * lax.rsqrt(jnp.mean(o * o, axis=0, keepdims=True) + SUBLN_EPS) * sg_ref[...]
    o_ref[...] = (o * (1.0 - lambda_init)).T.astype(BF16)


def _alibi_key_columns(S):
    j = np.arange(S, dtype=np.float64)
    cols = np.zeros((DIFF_HEADS, S, LANES), np.float32)
    for h, slope in enumerate(_alibi_slopes(DIFF_HEADS)):
        rest = slope * LOG2E * j
        for part in range(POS_PARTS):
            piece = rest.astype(ml_dtypes.bfloat16).astype(np.float64)
            cols[h, :, part] = piece
            rest = rest - piece
    return jnp.asarray(cols, BF16)


def _attn_a(qkv3, lam_qk, subln_g, lambda_init, B, S):
    T = B * S
    H = DIFF_HEADS
    tq = ATTN_QUERY_TILE
    tk = ATTN_KEY_TILE
    nq = S // tq
    kernel = functools.partial(_attn_a_kernel, lambda_init=lambda_init)
    return pl.pallas_call(
        kernel,
        grid=(B, H, nq),
        in_specs=[
            pl.BlockSpec((None, tq, LANES), lambda b, h, qi: (h, b * nq + qi, 0)),
            pl.BlockSpec((None, S, LANES), lambda b, h, qi: (H + h, b, 0)),
            pl.BlockSpec((None, S, LANES), lambda b, h, qi: (2 * H + h, b, 0)),
            pl.BlockSpec((None, S, LANES), lambda b, h, qi: (h, 0, 0)),
            _resident((4, HEAD_DIM)),
            _resident((LANES, 1)),
        ],
        out_specs=pl.BlockSpec((tq, LANES), lambda b, h, qi: (b * nq + qi, h)),
        out_shape=jax.ShapeDtypeStruct((T, D_MODEL), BF16),
        scratch_shapes=[
            pltpu.VMEM((S, 2 * LANES), BF16),
            pltpu.VMEM((S // tk, LANES + SUM_ROWS, tk), BF16),
            pltpu.VMEM((2 * LANES, 2 * tq), BF16),
            pltpu.VMEM((ATTN_SCORE_SLOTS, tk, tk), F32),
            pltpu.VMEM((1, 2 * tq), F32),
            pltpu.VMEM((LANES + SUM_ROWS, 2 * tq), F32),
        ],
        compiler_params=_params(3),
        name="diff_attn",
    )(qkv3, qkv3, qkv3, _alibi_key_columns(S), lam_qk, subln_g.reshape(LANES, 1))


def _attn_b_kernel(sinks_ref, q_ref, k_ref, v_ref, bias_ref, o_ref, s_scr):
    dim = lax.broadcasted_iota(jnp.int32, (LANES, WINDOW), 0)
    pairs_per_kv = SWA_GROUP // 2
    group_lanes = SWA_GROUP * WINDOW
    ones_rows = jnp.ones((SUM_ROWS, 2 * WINDOW), BF16)

    def window_start(qb):
        nb = pl.program_id(1) * SWA_BLOCKS_PER_STEP + qb
        return nb, pl.multiple_of(jnp.maximum(nb - 1, 0) * WINDOW, WINDOW)

    def scores(qb, kh):
        _, start = window_start(qb)
        kwin = k_ref[kh, pl.ds(start, 2 * WINDOW), :]
        cols = []
        for cpair in range(pairs_per_kv):
            qt = q_ref[kh * pairs_per_kv + cpair, qb * WINDOW:(qb + 1) * WINDOW, :].T
            zero = jnp.zeros_like(qt)
            cols.append(jnp.where(dim < HEAD_DIM, qt, zero))
            cols.append(jnp.where(dim >= HEAD_DIM, qt, zero))
        return jnp.dot(kwin, jnp.concatenate(cols, axis=1),
                       preferred_element_type=F32)

    def softmax_pv(qb, kh, s_ref):
        nb, start = window_start(qb)
        bias = bias_ref.at[jnp.minimum(nb, 1) if qb == 0 else 1]
        vt = v_ref[pl.ds(start, 2 * WINDOW), :].T[kh * HEAD_DIM:(kh + 1) * HEAD_DIM]
        v_ext = jnp.concatenate([vt, ones_rows], axis=0)
        for cpair in range(pairs_per_kv):
            lanes = slice(2 * cpair * WINDOW, (2 * cpair + 2) * WINDOW)
            sink = jnp.concatenate(
                [jnp.full((1, WINDOW), sinks_ref[kh * SWA_GROUP + 2 * cpair + g] * LOG2E, F32)
                 for g in range(2)], axis=1)
            b0 = kh * group_lanes + 2 * cpair * WINDOW
            s = s_ref[:, lanes] + bias[:, b0:b0 + 2 * WINDOW]
            m = jnp.maximum(jnp.max(s, axis=0, keepdims=True), sink)
            p = jnp.exp2(s - m).astype(BF16)
            o = jnp.dot(v_ext, p, preferred_element_type=F32)
            denom = o[HEAD_DIM:HEAD_DIM + 1] + jnp.exp2(sink - m)
            o = o[:HEAD_DIM] * (1.0 / denom)
            pair = jnp.concatenate([o[:, :WINDOW], o[:, WINDOW:]], axis=0)
            col = (kh * pairs_per_kv + cpair) * LANES
            o_ref[qb * WINDOW:(qb + 1) * WINDOW, col:col + LANES] = pair.T.astype(BF16)

    items = [(qb, kh) for qb in range(SWA_BLOCKS_PER_STEP) for kh in range(SWA_KV_HEADS)]
    s_scr[0] = scores(*items[0])
    for n, item in enumerate(items):
        if n + 1 < len(items):
            s_scr[(n + 1) % 2] = scores(*items[n + 1])
        softmax_pv(*item, s_scr.at[n % 2])


def _swa_bias():
    key = np.arange(2 * WINDOW)[:, None]
    qry = np.arange(WINDOW)[None, :]
    bias = np.full((2, 2 * WINDOW, SWA_Q_HEADS * WINDOW), -np.inf, np.float32)
    for case, offset in enumerate((0, WINDOW)):
        dist = offset + qry - key
        valid = (dist >= 0) & (dist < WINDOW)
        for h, slope in enumerate(_alibi_slopes(SWA_Q_HEADS)):
            bias[case, :, h * WINDOW:(h + 1) * WINDOW] = np.where(valid, -slope * LOG2E * dist, -np.inf)
    return jnp.asarray(bias)


def _attn_b(q3, k3, v3, sinks, B, S):
    T = B * S
    rows = SWA_BLOCKS_PER_STEP * WINDOW
    nstep = S // rows
    n_pairs = SWA_Q_HEADS // 2
    return pl.pallas_call(
        _attn_b_kernel,
        grid_spec=pltpu.PrefetchScalarGridSpec(
            num_scalar_prefetch=1,
            grid=(B, nstep),
            in_specs=[
                pl.BlockSpec((n_pairs, rows, LANES), lambda b, n, *_: (0, b * nstep + n, 0)),
                pl.BlockSpec((SWA_KV_HEADS, S, LANES), lambda b, n, *_: (0, b, 0)),
                pl.BlockSpec((None, S, LANES), lambda b, n, *_: (SWA_KV_HEADS, b, 0)),
                _resident((2, 2 * WINDOW, SWA_Q_HEADS * WINDOW)),
            ],
            out_specs=pl.BlockSpec((rows, D_MODEL), lambda b, n, *_: (b * nstep + n, 0)),
            scratch_shapes=[pltpu.VMEM((2, 2 * WINDOW, SWA_GROUP * WINDOW), F32)],
        ),
        out_shape=jax.ShapeDtypeStruct((T, D_MODEL), BF16),
        compiler_params=_params(2),
        name="swa_attn",
    )(sinks, q3, k3, v3, _swa_bias())


def _gelu_tanh_of_half(h):
    c = math.sqrt(2.0 / math.pi)
    return h + h * jnp.tanh(h * (2.0 * c + (8.0 * 0.044715 * c) * (h * h)))


def _ffn_kernel(o_ref, x_ref, g1_ref, g2_ref, g3_ref, wo_ref, wgu_ref, wd_ref, cw_ref, cb_ref, out_ref,
                x1_scr, a_scr, carry_scr, gate0_scr, gate1_scr, up0_scr, up1_scr, *, tiles_per_seq):
    i = pl.program_id(0)
    tm = x_ref.shape[0]
    rb = FF_ROW_BLOCK
    n_blocks = tm // rb
    tf = FF_CHUNK
    slots = ((gate0_scr, up0_scr), (gate1_scr, up1_scr))

    @pl.when(i == 0)
    def _():
        carry_scr[...] = jnp.zeros(carry_scr.shape, F32)

    seq_start = (i % tiles_per_seq) == 0

    def up_proj(r):
        gate_scr, up_scr = slots[r % 2]
        rows = slice(r * rb, (r + 1) * rb)
        if r == 0:
            gate_scr[0:8, :] = jnp.where(seq_start, 0.0, carry_scr[...])
        else:
            gate_scr[0:8, :] = slots[(r - 1) % 2][0][rb:rb + 8, :]
        mix = jnp.dot(o_ref[rows, :], wo_ref[...], preferred_element_type=F32)
        x1 = x_ref[rows, :] + _rms(mix, g1_ref[...], NORM_EPS)
        x1_scr[rows, :] = x1
        h = _rms(x1, g2_ref[...], NORM_EPS).astype(BF16)
        for c in range(0, D_FF, FF_DOT_COLS):
            w = min(FF_DOT_COLS, D_FF - c)
            gate_scr[8:8 + rb, c:c + w] = jnp.dot(h, wgu_ref[:, c:c + w], preferred_element_type=F32)
            up_scr[:, c:c + w] = jnp.dot(h, wgu_ref[:, D_FF + c:D_FF + c + w],
                                         preferred_element_type=F32)

    def act_down(r):
        gate_scr, up_scr = slots[r % 2]
        rows = slice(r * rb, (r + 1) * rb)
        for c in range(0, D_FF, tf):
            cols = slice(c, c + tf)
            cw = 0.5 * cw_ref[:, cols]
            half_conv = (cw[2:3] * gate_scr[8:8 + rb, cols] + 0.5 * cb_ref[:, cols]
                         + cw[1:2] * gate_scr[7:7 + rb, cols]
                         + cw[0:1] * gate_scr[6:6 + rb, cols])
            a_scr[:, cols] = (_gelu_tanh_of_half(half_conv) * up_scr[:, cols]).astype(BF16)
        f = jnp.dot(a_scr[...], wd_ref[...], preferred_element_type=F32)
        out_ref[rows, :] = x1_scr[rows, :] + _rms(f, g3_ref[...], NORM_EPS)
        if r == n_blocks - 1:
            carry_scr[...] = gate_scr[rb:rb + 8, :]

    up_proj(0)
    for r in range(n_blocks):
        if r + 1 < n_blocks:
            up_proj(r + 1)
        act_down(r)


def _post_ffn(o, x, g1, g2, g3, w_o, w_up, conv_w, conv_b, w_down, S, in_place):
    T, D = x.shape
    tm = FF_ROW_TILE
    rb = FF_ROW_BLOCK
    kernel = functools.partial(_ffn_kernel, tiles_per_seq=S // tm)
    return pl.pallas_call(
        kernel,
        grid=(T // tm,),
        in_specs=[
            pl.BlockSpec((tm, D), lambda i: (i, 0)),
            pl.BlockSpec((tm, D), lambda i: (i, 0)),
            _resident((1, D)),
            _resident((1, D)),
            _resident((1, D)),
            _resident((D, D)),
            _resident((D, 2 * D_FF)),
            _resident((D_FF, D)),
            _resident((CONV_WIDTH, D_FF)),
            _resident((1, D_FF)),
        ],
        out_specs=pl.BlockSpec((tm, D), lambda i: (i, 0)),
        out_shape=jax.ShapeDtypeStruct((T, D), F32),
        scratch_shapes=[
            pltpu.VMEM((tm, D), F32),
            pltpu.VMEM((rb, D_FF), BF16),
            pltpu.VMEM((8, D_FF), F32),
            pltpu.VMEM((rb + 8, D_FF), F32),
            pltpu.VMEM((rb + 8, D_FF), F32),
            pltpu.VMEM((rb, D_FF), F32),
            pltpu.VMEM((rb, D_FF), F32),
        ],
        input_output_aliases={1: 0} if in_place else {},
        compiler_params=_params(1),
        name="post_ffn",
    )(o, x, g1.reshape(1, D), g2.reshape(1, D), g3.reshape(1, D), w_o.astype(BF16),
      w_up.astype(BF16), w_down.astype(BF16), conv_w, conv_b.reshape(1, D_FF))


def _dup_heads(w, n_heads):
    lead = w.shape[:-1]
    w = w.reshape(lead + (n_heads, 1, HEAD_DIM))
    return jnp.broadcast_to(w, lead + (n_heads, 2, HEAD_DIM)).reshape(lead + (2 * n_heads * HEAD_DIM,))


def kernel(x, norm_gains, w_qkv_a, lambda_qk_a, subln_a, w_o_a, kv_norm, w_kv_b, b_kv_b, w_q_b, b_q_b,
           sinks_b, w_o_b, w_up, conv_w, conv_b, w_down):
    B, S, D = x.shape
    T = B * S
    assert D == D_MODEL and S % ATTN_QUERY_TILE == 0 and S % ROW_TILE == 0 and S % FF_ROW_TILE == 0 and S >= 2 * WINDOW
    xt = x.reshape(T, D)
    kv3 = None
    for layer in range(DEPTH):
        g = norm_gains[layer]
        if layer < N_A_LAYERS:
            lambda_init = 0.8 - 0.6 * math.exp(-0.3 * layer)
            qkv3, = _proj(xt, [(g[0], w_qkv_a[layer].astype(BF16), jnp.zeros((3 * D,), F32),
                                DIFF_HEADS, QK_SCALE * LOG2E)])
            o = _attn_a(qkv3, lambda_qk_a[layer], subln_a[layer], lambda_init, B, S)
            w_o = w_o_a[layer]
        else:
            j = layer - N_A_LAYERS
            q_group = (g[0], w_q_b[j].astype(BF16), b_q_b[j], SWA_Q_HEADS // 2, QK_SCALE * LOG2E)
            if kv3 is None:
                kvw = SWA_KV_HEADS * HEAD_DIM
                w_kv = jnp.concatenate([_dup_heads(w_kv_b[:, :kvw], SWA_KV_HEADS), w_kv_b[:, kvw:]], axis=1)
                b_kv = jnp.concatenate([_dup_heads(b_kv_b[:kvw], SWA_KV_HEADS), b_kv_b[kvw:]])
                q3, kv3 = _proj(xt, [q_group, (kv_norm, w_kv.astype(BF16), b_kv, 0, 1.0)])
            else:
                q3, = _proj(xt, [q_group])
            o = _attn_b(q3, kv3, kv3, sinks_b[j], B, S)
            w_o = w_o_b[j]
        xt = _post_ffn(o, xt, g[1], g[2], g[3], w_o, w_up[layer], conv_w[layer], conv_b[layer],
                       w_down[layer], S, in_place=layer > 0)
    return xt.reshape(B, S, D)
```

```python
import functools
import math

import ml_dtypes
import numpy as np
import jax
import jax.numpy as jnp
from jax import lax
from jax.experimental import pallas as pl
from jax.experimental.pallas import tpu as pltpu

D_MODEL = 1024
HEAD_DIM = 64
DEPTH = 4
N_A_LAYERS = DEPTH // 2
DIFF_HEADS = D_MODEL // (2 * HEAD_DIM)
SWA_Q_HEADS = D_MODEL // HEAD_DIM
SWA_KV_HEADS = 2
SWA_GROUP = SWA_Q_HEADS // SWA_KV_HEADS
WINDOW = 128
D_FF = 2816
CONV_WIDTH = 3
NORM_EPS = 1e-6
SUBLN_EPS = 1e-5

LANES = 128
QK_SCALE = HEAD_DIM ** -0.5
VMEM_LIMIT_BYTES = 56 * 1024 * 1024
PROJ_DOT_SLABS = 4
ROW_TILE = 1024
ATTN_QUERY_TILE = 2048
ATTN_KEY_TILE = 512
ATTN_SCORES_AHEAD = 1
ATTN_SCORE_SLOTS = 2
SUM_ROWS = 16
SWA_BLOCKS_PER_STEP = 8
FF_CHUNK = 256
FF_DOT_COLS = 256
FF_ROW_TILE = 512
FF_ROW_BLOCK = 256
POS_PARTS = 3
LOG2E = math.log2(math.e)

BF16 = jnp.bfloat16
F32 = jnp.float32


def _alibi_slopes(n):
    return [2.0 ** (-8.0 * (i + 1) / n) for i in range(n)]


def _rms(x, g, eps):
    return x * lax.rsqrt(jnp.mean(x * x, axis=-1, keepdims=True) + eps) * g


def _params(n_axes):
    return pltpu.CompilerParams(
        dimension_semantics=("arbitrary",) * n_axes, vmem_limit_bytes=VMEM_LIMIT_BYTES)


def _resident(shape):
    nd = len(shape)
    return pl.BlockSpec(shape, lambda *_: (0,) * nd, pipeline_mode=pl.Buffered(1))


def _proj_kernel(x_ref, *refs, scaled):
    n = len(scaled)
    x = x_ref[...]
    xn = x * lax.rsqrt(jnp.mean(x * x, axis=-1, keepdims=True) + NORM_EPS)
    for k, (n_scaled, scale) in enumerate(scaled):
        g_ref, w_ref, b_ref = refs[3 * k:3 * k + 3]
        o_ref = refs[3 * n + k]
        n_slabs = o_ref.shape[0]
        h = (xn * g_ref[...]).astype(BF16)
        for c0 in range(0, n_slabs, PROJ_DOT_SLABS):
            nc = min(PROJ_DOT_SLABS, n_slabs - c0)
            cols = slice(c0 * LANES, (c0 + nc) * LANES)
            y = jnp.dot(h, w_ref[:, cols], preferred_element_type=F32) + b_ref[:, cols]
            for c in range(nc):
                slab = y[:, c * LANES:(c + 1) * LANES]
                if c0 + c < n_scaled:
                    slab = slab * scale
                o_ref[c0 + c] = slab.astype(BF16)


def _proj(x, groups):
    T, D = x.shape
    tm = ROW_TILE
    operands, in_specs, out_specs, out_shapes = [x], [pl.BlockSpec((tm, D), lambda i: (i, 0))], [], []
    for g, w, b, _, _ in groups:
        N = w.shape[1]
        operands += [g.reshape(1, D), w, b.reshape(1, N)]
        in_specs += [_resident((1, D)), _resident((D, N)), _resident((1, N))]
        out_specs.append(pl.BlockSpec((N // LANES, tm, LANES), lambda i: (0, i, 0)))
        out_shapes.append(jax.ShapeDtypeStruct((N // LANES, T, LANES), BF16))
    return pl.pallas_call(
        functools.partial(_proj_kernel, scaled=tuple((ns, sc) for *_, ns, sc in groups)),
        grid=(T // tm,),
        in_specs=in_specs,
        out_specs=out_specs,
        out_shape=out_shapes,
        compiler_params=_params(1),
        name="norm_proj",
    )(*operands)


def _attn_a_kernel(q_ref, k_ref, v_ref, kpos_ref, lq_ref, sg_ref, o_ref,
                   kk_scr, vt_scr, qq_scr, s_scr, m_scr, acc_scr, *, lambda_init):
    qi = pl.program_id(2)
    tq = q_ref.shape[0]
    tk = ATTN_KEY_TILE
    halves = tq // tk
    n_chunks = 2 * halves
    assert n_chunks % 2 == 0

    @pl.when(qi == 0)
    def _():
        kk_scr[:, :LANES] = k_ref[...]
        kk_scr[:, LANES:] = kpos_ref[...]
        for t in range(vt_scr.shape[0]):
            vt_scr[t, :LANES, :] = v_ref[t * tk:(t + 1) * tk, :].T
            vt_scr[t, LANES:, :] = jnp.ones((SUM_ROWS, tk), BF16)

    lane = lax.broadcasted_iota(jnp.int32, (tk, LANES), 1)
    for c in range(n_chunks):
        q = q_ref[(c % halves) * tk:(c % halves + 1) * tk, :]
        keep = (lane < HEAD_DIM) if c < halves else (lane >= HEAD_DIM)
        qq_scr[:LANES, c * tk:(c + 1) * tk] = jnp.where(keep, q, jnp.zeros_like(q)).T
    row = lax.broadcasted_iota(jnp.int32, (LANES, 2 * tq), 0)
    qq_scr[LANES:, :] = jnp.where(row < POS_PARTS, 1.0, 0.0).astype(BF16)

    m_scr[...] = jnp.full(m_scr.shape, -jnp.inf, F32)
    acc_scr[...] = jnp.zeros(acc_scr.shape, F32)

    def scores(ki, c, slot):
        start = pl.multiple_of(ki * tk, tk)
        s_scr[slot] = jnp.dot(kk_scr[pl.ds(start, tk), :], qq_scr[:, c * tk:(c + 1) * tk],
                              preferred_element_type=F32)

    def softmax_pv(ki, c, slot, triangular):
        cols = slice(c * tk, (c + 1) * tk)
        s = s_scr[slot]
        if triangular:
            key = lax.broadcasted_iota(jnp.int32, (tk, tk), 0)
            qry = lax.broadcasted_iota(jnp.int32, (tk, tk), 1)
            s = jnp.where(key <= qry, s, -jnp.inf)
        m_prev = m_scr[:, cols]
        m_new = jnp.maximum(m_prev, jnp.max(s, axis=0, keepdims=True))
        alpha = jnp.exp2(m_prev - m_new)
        p = jnp.exp2(s - m_new)
        acc_scr[:, cols] = alpha * acc_scr[:, cols] + jnp.dot(
            vt_scr[ki], p.astype(BF16), preferred_element_type=F32)
        m_scr[:, cols] = m_new

    ahead = ATTN_SCORES_AHEAD
    n_slots = s_scr.shape[0]
    assert n_chunks % n_slots == 0 and ahead < n_slots and ahead <= n_chunks
    first_diag = halves * qi
    items = [(d, c) for d in range(halves) for c in range(n_chunks) if c % halves >= d]
    assert items[:ahead] == [(0, c) for c in range(ahead)]

    for c in range(ahead):
        scores(0, c, c)

    def body(ki, carry):
        for c in range(n_chunks):
            nxt = c + ahead
            scores(ki + nxt // n_chunks, nxt % n_chunks, nxt % n_slots)
            softmax_pv(ki, c, c % n_slots, False)
        return carry

    def two_tiles(kp, carry):
        return body(2 * kp + 1, body(2 * kp, carry))

    assert halves % 2 == 0
    lax.fori_loop(0, first_diag // 2, two_tiles, 0)

    for n, (d, c) in enumerate(items):
        if n + ahead < len(items):
            d_next, c_next = items[n + ahead]
            scores(first_diag + d_next, c_next, (n + ahead) % n_slots)
        softmax_pv(first_diag + d, c, n % n_slots, c % halves == d)

    lq = lq_ref[...]
    lam = (jnp.exp(jnp.sum(lq[0:1] * lq[1:2], axis=-1, keepdims=True))
           - jnp.exp(jnp.sum(lq[2:3] * lq[3:4], axis=-1, keepdims=True)) + lambda_init)
    l = acc_scr[LANES:LANES + 1, :]
    o = (acc_scr[:LANES, :tq] / l[:, :tq] - lam * (acc_scr[:LANES, tq:] / l[:, tq:]))
    o = o * lax.rsqrt(jnp.mean(o * o, axis=0, keepdims=True) + SUBLN_EPS) * sg_ref[...]
    o_ref[...] = (o * (1.0 - lambda_init)).T.astype(BF16)


def _alibi_key_columns(S):
    j = np.arange(S, dtype=np.float64)
    cols = np.zeros((DIFF_HEADS, S, LANES), np.float32)
    for h, slope in enumerate(_alibi_slopes(DIFF_HEADS)):
        rest = slope * LOG2E * j
        for part in range(POS_PARTS):
            piece = rest.astype(ml_dtypes.bfloat16).astype(np.float64)
            cols[h, :, part] = piece
            rest = rest - piece
    return jnp.asarray(cols, BF16)


def _attn_a(qkv3, lam_qk, subln_g, lambda_init, B, S):
    T = B * S
    H = DIFF_HEADS
    tq = ATTN_QUERY_TILE
    tk = ATTN_KEY_TILE
    nq = S // tq
    kernel = functools.partial(_attn_a_kernel, lambda_init=lambda_init)
    return pl.pallas_call(
        kernel,
        grid=(B, H, nq),
        in_specs=[
            pl.BlockSpec((None, tq, LANES), lambda b, h, qi: (h, b * nq + qi, 0)),
            pl.BlockSpec((None, S, LANES), lambda b, h, qi: (H + h, b, 0)),
            pl.BlockSpec((None, S, LANES), lambda b, h, qi: (2 * H + h, b, 0)),
            pl.BlockSpec((None, S, LANES), lambda b, h, qi: (h, 0, 0)),
            _resident((4, HEAD_DIM)),
            _resident((LANES, 1)),
        ],
        out_specs=pl.BlockSpec((tq, LANES), lambda b, h, qi: (b * nq + qi, h)),
        out_shape=jax.ShapeDtypeStruct((T, D_MODEL), BF16),
        scratch_shapes=[
            pltpu.VMEM((S, 2 * LANES), BF16),
            pltpu.VMEM((S // tk, LANES + SUM_ROWS, tk), BF16),
            pltpu.VMEM((2 * LANES, 2 * tq), BF16),
            pltpu.VMEM((ATTN_SCORE_SLOTS, tk, tk), F32),
            pltpu.VMEM((1, 2 * tq), F32),
            pltpu.VMEM((LANES + SUM_ROWS, 2 * tq), F32),
        ],
        compiler_params=_params(3),
        name="diff_attn",
    )(qkv3, qkv3, qkv3, _alibi_key_columns(S), lam_qk, subln_g.reshape(LANES, 1))


def _attn_b_kernel(sinks_ref, q_ref, k_ref, v_ref, bias_ref, o_ref, s_scr):
    dim = lax.broadcasted_iota(jnp.int32, (LANES, WINDOW), 0)
    pairs_per_kv = SWA_GROUP // 2
    group_lanes = SWA_GROUP * WINDOW
    ones_rows = jnp.ones((SUM_ROWS, 2 * WINDOW), BF16)

    def window_start(qb):
        nb = pl.program_id(1) * SWA_BLOCKS_PER_STEP + qb
        return nb, pl.multiple_of(jnp.maximum(nb - 1, 0) * WINDOW, WINDOW)

    def scores(qb, kh):
        _, start = window_start(qb)
        kwin = k_ref[kh, pl.ds(start, 2 * WINDOW), :]
        cols = []
        for cpair in range(pairs_per_kv):
            qt = q_ref[kh * pairs_per_kv + cpair, qb * WINDOW:(qb + 1) * WINDOW, :].T
            zero = jnp.zeros_like(qt)
            cols.append(jnp.where(dim < HEAD_DIM, qt, zero))
            cols.append(jnp.where(dim >= HEAD_DIM, qt, zero))
        return jnp.dot(kwin, jnp.concatenate(cols, axis=1),
                       preferred_element_type=F32)

    def softmax_pv(qb, kh, s_ref):
        nb, start = window_start(qb)
        bias = bias_ref.at[jnp.minimum(nb, 1) if qb == 0 else 1]
        vt = v_ref[pl.ds(start, 2 * WINDOW), :].T[kh * HEAD_DIM:(kh + 1) * HEAD_DIM]
        v_ext = jnp.concatenate([vt, ones_rows], axis=0)
        for cpair in range(pairs_per_kv):
            lanes = slice(2 * cpair * WINDOW, (2 * cpair + 2) * WINDOW)
            sink = jnp.concatenate(
                [jnp.full((1, WINDOW), sinks_ref[kh * SWA_GROUP + 2 * cpair + g] * LOG2E, F32)
                 for g in range(2)], axis=1)
            b0 = kh * group_lanes + 2 * cpair * WINDOW
            s = s_ref[:, lanes] + bias[:, b0:b0 + 2 * WINDOW]
            m = jnp.maximum(jnp.max(s, axis=0, keepdims=True), sink)
            p = jnp.exp2(s - m).astype(BF16)
            o = jnp.dot(v_ext, p, preferred_element_type=F32)
            denom = o[HEAD_DIM:HEAD_DIM + 1] + jnp.exp2(sink - m)
            o = o[:HEAD_DIM] * (1.0 / denom)
            pair = jnp.concatenate([o[:, :WINDOW], o[:, WINDOW:]], axis=0)
            col = (kh * pairs_per_kv + cpair) * LANES
            o_ref[qb * WINDOW:(qb + 1) * WINDOW, col:col + LANES] = pair.T.astype(BF16)

    items = [(qb, kh) for qb in range(SWA_BLOCKS_PER_STEP) for kh in range(SWA_KV_HEADS)]
    s_scr[0] = scores(*items[0])
    for n, item in enumerate(items):
        if n + 1 < len(items):
            s_scr[(n + 1) % 2] = scores(*items[n + 1])
        softmax_pv(*item, s_scr.at[n % 2])


def _swa_bias():
    key = np.arange(2 * WINDOW)[:, None]
    qry = np.arange(WINDOW)[None, :]
    bias = np.full((2, 2 * WINDOW, SWA_Q_HEADS * WINDOW), -np.inf, np.float32)
    for case, offset in enumerate((0, WINDOW)):
        dist = offset + qry - key
        valid = (dist >= 0) & (dist < WINDOW)
        for h, slope in enumerate(_alibi_slopes(SWA_Q_HEADS)):
            bias[case, :, h * WINDOW:(h + 1) * WINDOW] = np.where(valid, -slope * LOG2E * dist, -np.inf)
    return jnp.asarray(bias)


def _attn_b(q3, k3, v3, sinks, B, S):
    T = B * S
    rows = SWA_BLOCKS_PER_STEP * WINDOW
    nstep = S // rows
    n_pairs = SWA_Q_HEADS // 2
    return pl.pallas_call(
        _attn_b_kernel,
        grid_spec=pltpu.PrefetchScalarGridSpec(
            num_scalar_prefetch=1,
            grid=(B, nstep),
            in_specs=[
                pl.BlockSpec((n_pairs, rows, LANES), lambda b, n, *_: (0, b * nstep + n, 0)),
                pl.BlockSpec((SWA_KV_HEADS, S, LANES), lambda b, n, *_: (0, b, 0)),
                pl.BlockSpec((None, S, LANES), lambda b, n, *_: (SWA_KV_HEADS, b, 0)),
                _resident((2, 2 * WINDOW, SWA_Q_HEADS * WINDOW)),
            ],
            out_specs=pl.BlockSpec((rows, D_MODEL), lambda b, n, *_: (b * nstep + n, 0)),
            scratch_shapes=[pltpu.VMEM((2, 2 * WINDOW, SWA_GROUP * WINDOW), F32)],
        ),
        out_shape=jax.ShapeDtypeStruct((T, D_MODEL), BF16),
        compiler_params=_params(2),
        name="swa_attn",
    )(sinks, q3, k3, v3, _swa_bias())


def _gelu_tanh_of_half(h):
    c = math.sqrt(2.0 / math.pi)
    return h + h * jnp.tanh(h * (2.0 * c + (8.0 * 0.044715 * c) * (h * h)))


def _ffn_kernel(o_ref, x_ref, g1_ref, g2_ref, g3_ref, wo_ref, wgu_ref, wd_ref, cw_ref, cb_ref, out_ref,
                x1_scr, h_scr, a_scr, carry_scr, gate0_scr, gate1_scr, up0_scr, up1_scr,
                *, tiles_per_seq):
    i = pl.program_id(0)
    tm = x_ref.shape[0]
    rb = FF_ROW_BLOCK
    n_blocks = tm // rb
    tf = FF_CHUNK
    slots = ((gate0_scr, up0_scr), (gate1_scr, up1_scr))

    @pl.when(i == 0)
    def _():
        carry_scr[...] = jnp.zeros(carry_scr.shape, F32)

    seq_start = (i % tiles_per_seq) == 0

    def mix_norm(r):
        rows = slice(r * rb, (r + 1) * rb)
        mix = jnp.dot(o_ref[rows, :], wo_ref[...], preferred_element_type=F32)
        x1 = x_ref[rows, :] + _rms(mix, g1_ref[...], NORM_EPS)
        x1_scr[rows, :] = x1
        h_scr[rows, :] = _rms(x1, g2_ref[...], NORM_EPS).astype(BF16)

    def up_proj(r):
        gate_scr, up_scr = slots[r % 2]
        if r == 0:
            gate_scr[0:8, :] = jnp.where(seq_start, 0.0, carry_scr[...])
        else:
            gate_scr[0:8, :] = slots[(r - 1) % 2][0][rb:rb + 8, :]
        h = h_scr[r * rb:(r + 1) * rb, :]
        for c in range(0, D_FF, FF_DOT_COLS):
            w = min(FF_DOT_COLS, D_FF - c)
            gate_scr[8:8 + rb, c:c + w] = jnp.dot(h, wgu_ref[:, c:c + w], preferred_element_type=F32)
            up_scr[:, c:c + w] = jnp.dot(h, wgu_ref[:, D_FF + c:D_FF + c + w],
                                         preferred_element_type=F32)

    def act_down(r):
        gate_scr, up_scr = slots[r % 2]
        rows = slice(r * rb, (r + 1) * rb)
        for c in range(0, D_FF, tf):
            cols = slice(c, c + tf)
            cw = 0.5 * cw_ref[:, cols]
            half_conv = (cw[2:3] * gate_scr[8:8 + rb, cols] + 0.5 * cb_ref[:, cols]
                         + cw[1:2] * gate_scr[7:7 + rb, cols]
                         + cw[0:1] * gate_scr[6:6 + rb, cols])
            a_scr[:, cols] = (_gelu_tanh_of_half(half_conv) * up_scr[:, cols]).astype(BF16)
        f = jnp.dot(a_scr[...], wd_ref[...], preferred_element_type=F32)
        out_ref[rows, :] = x1_scr[rows, :] + _rms(f, g3_ref[...], NORM_EPS)
        if r == n_blocks - 1:
            carry_scr[...] = gate_scr[rb:rb + 8, :]

    for r in range(n_blocks):
        mix_norm(r)
    up_proj(0)
    for r in range(n_blocks):
        if r + 1 < n_blocks:
            up_proj(r + 1)
        act_down(r)


def _post_ffn(o, x, g1, g2, g3, w_o, w_up, conv_w, conv_b, w_down, S, in_place):
    T, D = x.shape
    tm = FF_ROW_TILE
    rb = FF_ROW_BLOCK
    kernel = functools.partial(_ffn_kernel, tiles_per_seq=S // tm)
    return pl.pallas_call(
        kernel,
        grid=(T // tm,),
        in_specs=[
            pl.BlockSpec((tm, D), lambda i: (i, 0)),
            pl.BlockSpec((tm, D), lambda i: (i, 0)),
            _resident((1, D)),
            _resident((1, D)),
            _resident((1, D)),
            _resident((D, D)),
            _resident((D, 2 * D_FF)),
            _resident((D_FF, D)),
            _resident((CONV_WIDTH, D_FF)),
            _resident((1, D_FF)),
        ],
        out_specs=pl.BlockSpec((tm, D), lambda i: (i, 0)),
        out_shape=jax.ShapeDtypeStruct((T, D), F32),
        scratch_shapes=[
            pltpu.VMEM((tm, D), F32),
            pltpu.VMEM((tm, D), BF16),
            pltpu.VMEM((rb, D_FF), BF16),
            pltpu.VMEM((8, D_FF), F32),
            pltpu.VMEM((rb + 8, D_FF), F32),
            pltpu.VMEM((rb + 8, D_FF), F32),
            pltpu.VMEM((rb, D_FF), F32),
            pltpu.VMEM((rb, D_FF), F32),
        ],
        input_output_aliases={1: 0} if in_place else {},
        compiler_params=_params(1),
        name="post_ffn",
    )(o, x, g1.reshape(1, D), g2.reshape(1, D), g3.reshape(1, D), w_o.astype(BF16),
      w_up.astype(BF16), w_down.astype(BF16), conv_w, conv_b.reshape(1, D_FF))


def _dup_heads(w, n_heads):
    lead = w.shape[:-1]
    w = w.reshape(lead + (n_heads, 1, HEAD_DIM))
    return jnp.broadcast_to(w, lead + (n_heads, 2, HEAD_DIM)).reshape(lead + (2 * n_heads * HEAD_DIM,))


def kernel(x, norm_gains, w_qkv_a, lambda_qk_a, subln_a, w_o_a, kv_norm, w_kv_b, b_kv_b, w_q_b, b_q_b,
           sinks_b, w_o_b, w_up, conv_w, conv_b, w_down):
    B, S, D = x.shape
    T = B * S
    assert D == D_MODEL and S % ATTN_QUERY_TILE == 0 and S % ROW_TILE == 0 and S % FF_ROW_TILE == 0 and S >= 2 * WINDOW
    xt = x.reshape(T, D)
    kv3 = None
    for layer in range(DEPTH):
        g = norm_gains[layer]
        if layer < N_A_LAYERS:
            lambda_init = 0.8 - 0.6 * math.exp(-0.3 * layer)
            qkv3, = _proj(xt, [(g[0], w_qkv_a[layer].astype(BF16), jnp.zeros((3 * D,), F32),
                                DIFF_HEADS, QK_SCALE * LOG2E)])
            o = _attn_a(qkv3, lambda_qk_a[layer], subln_a[layer], lambda_init, B, S)
            w_o = w_o_a[layer]
        else:
            j = layer - N_A_LAYERS
            q_group = (g[0], w_q_b[j].astype(BF16), b_q_b[j], SWA_Q_HEADS // 2, QK_SCALE * LOG2E)
            if kv3 is None:
                kvw = SWA_KV_HEADS * HEAD_DIM
                w_kv = jnp.concatenate([_dup_heads(w_kv_b[:, :kvw], SWA_KV_HEADS), w_kv_b[:, kvw:]], axis=1)
                b_kv = jnp.concatenate([_dup_heads(b_kv_b[:kvw], SWA_KV_HEADS), b_kv_b[kvw:]])
                q3, kv3 = _proj(xt, [q_group, (kv_norm, w_kv.astype(BF16), b_kv, 0, 1.0)])
            else:
                q3, = _proj(xt, [q_group])
            o = _attn_b(q3, kv3, kv3, sinks_b[j], B, S)
            w_o = w_o_b[j]
        xt = _post_ffn(o, xt, g[1], g[2], g[3], w_o, w_up[layer], conv_w[layer], conv_b[layer],
                       w_down[layer], S, in_place=layer > 0)
    return xt.reshape(B, S, D)
```

```python
import functools
import math

import ml_dtypes
import numpy as np
import jax
import jax.numpy as jnp
from jax import lax
from jax.experimental import pallas as pl
from jax.experimental.pallas import tpu as pltpu

D_MODEL = 1024
HEAD_DIM = 64
DEPTH = 4
N_A_LAYERS = DEPTH // 2
DIFF_HEADS = D_MODEL // (2 * HEAD_DIM)
SWA_Q_HEADS = D_MODEL // HEAD_DIM
SWA_KV_HEADS = 2
SWA_GROUP = SWA_Q_HEADS // SWA_KV_HEADS
WINDOW = 128
D_FF = 2816
CONV_WIDTH = 3
NORM_EPS = 1e-6
SUBLN_EPS = 1e-5

LANES = 128
QK_SCALE = HEAD_DIM ** -0.5
VMEM_LIMIT_BYTES = 56 * 1024 * 1024
PROJ_DOT_SLABS = 4
ROW_TILE = 1024
ATTN_QUERY_TILE = 2048
ATTN_KEY_TILE = 512
ATTN_SCORES_AHEAD = 1
ATTN_SCORE_SLOTS = 2
SUM_ROWS = 16
SWA_BLOCKS_PER_STEP = 8
FF_CHUNK = 256
FF_DOT_COLS = 256
FF_ROW_TILE = 1024
FF_ROW_BLOCK = 256
POS_PARTS = 3
LOG2E = math.log2(math.e)

BF16 = jnp.bfloat16
F32 = jnp.float32


def _alibi_slopes(n):
    return [2.0 ** (-8.0 * (i + 1) / n) for i in range(n)]


def _rms(x, g, eps):
    return x * lax.rsqrt(jnp.mean(x * x, axis=-1, keepdims=True) + eps) * g


def _params(n_axes):
    return pltpu.CompilerParams(
        dimension_semantics=("arbitrary",) * n_axes, vmem_limit_bytes=VMEM_LIMIT_BYTES)


def _resident(shape):
    nd = len(shape)
    return pl.BlockSpec(shape, lambda *_: (0,) * nd, pipeline_mode=pl.Buffered(1))


def _proj_kernel(x_ref, *refs, scaled):
    n = len(scaled)
    x = x_ref[...]
    xn = x * lax.rsqrt(jnp.mean(x * x, axis=-1, keepdims=True) + NORM_EPS)
    for k, (n_scaled, scale, transposed) in enumerate(scaled):
        g_ref, w_ref, b_ref = refs[3 * k:3 * k + 3]
        o_ref = refs[3 * n + k]
        n_slabs = o_ref.shape[0]
        h = (xn * g_ref[...]).astype(BF16)
        for c0 in range(0, n_slabs, PROJ_DOT_SLABS):
            nc = min(PROJ_DOT_SLABS, n_slabs - c0)
            cols = slice(c0 * LANES, (c0 + nc) * LANES)
            y = jnp.dot(h, w_ref[:, cols], preferred_element_type=F32) + b_ref[:, cols]
            for c in range(nc):
                slab = y[:, c * LANES:(c + 1) * LANES]
                if c0 + c < n_scaled:
                    slab = slab * scale
                o_ref[c0 + c] = (slab.T if transposed else slab).astype(BF16)


def _proj(x, groups):
    T, D = x.shape
    tm = ROW_TILE
    operands, in_specs, out_specs, out_shapes = [x], [pl.BlockSpec((tm, D), lambda i: (i, 0))], [], []
    for g, w, b, _, _, transposed in groups:
        N = w.shape[1]
        operands += [g.reshape(1, D), w, b.reshape(1, N)]
        in_specs += [_resident((1, D)), _resident((D, N)), _resident((1, N))]
        if transposed:
            out_specs.append(pl.BlockSpec((N // LANES, LANES, tm), lambda i: (0, 0, i)))
            out_shapes.append(jax.ShapeDtypeStruct((N // LANES, LANES, T), BF16))
        else:
            out_specs.append(pl.BlockSpec((N // LANES, tm, LANES), lambda i: (0, i, 0)))
            out_shapes.append(jax.ShapeDtypeStruct((N // LANES, T, LANES), BF16))
    return pl.pallas_call(
        functools.partial(_proj_kernel, scaled=tuple(g[3:] for g in groups)),
        grid=(T // tm,),
        in_specs=in_specs,
        out_specs=out_specs,
        out_shape=out_shapes,
        compiler_params=_params(1),
        name="norm_proj",
    )(*operands)


def _attn_a_kernel(q_ref, k_ref, v_ref, kpos_ref, lq_ref, sg_ref, o_ref,
                   kk_scr, vt_scr, qq_scr, s_scr, m_scr, acc_scr, *, lambda_init):
    qi = pl.program_id(2)
    tq = q_ref.shape[1]
    tk = ATTN_KEY_TILE
    halves = tq // tk
    n_chunks = 2 * halves
    assert n_chunks % 2 == 0

    @pl.when(qi == 0)
    def _():
        kk_scr[:, :LANES] = k_ref[...]
        kk_scr[:, LANES:] = kpos_ref[...]
        for t in range(vt_scr.shape[0]):
            vt_scr[t, :LANES, :] = v_ref[:, t * tk:(t + 1) * tk]
            vt_scr[t, LANES:, :] = jnp.ones((SUM_ROWS, tk), BF16)

    q = q_ref[...]
    dim = lax.broadcasted_iota(jnp.int32, (LANES, tq), 0)
    qq_scr[:LANES, :tq] = jnp.where(dim < HEAD_DIM, q, jnp.zeros_like(q))
    qq_scr[:LANES, tq:] = jnp.where(dim >= HEAD_DIM, q, jnp.zeros_like(q))
    row = lax.broadcasted_iota(jnp.int32, (LANES, 2 * tq), 0)
    qq_scr[LANES:, :] = jnp.where(row < POS_PARTS, 1.0, 0.0).astype(BF16)

    m_scr[...] = jnp.full(m_scr.shape, -jnp.inf, F32)
    acc_scr[...] = jnp.zeros(acc_scr.shape, F32)

    def scores(ki, c, slot):
        start = pl.multiple_of(ki * tk, tk)
        s_scr[slot] = jnp.dot(kk_scr[pl.ds(start, tk), :], qq_scr[:, c * tk:(c + 1) * tk],
                              preferred_element_type=F32)

    def softmax_pv(ki, c, slot, triangular):
        cols = slice(c * tk, (c + 1) * tk)
        s = s_scr[slot]
        if triangular:
            key = lax.broadcasted_iota(jnp.int32, (tk, tk), 0)
            qry = lax.broadcasted_iota(jnp.int32, (tk, tk), 1)
            s = jnp.where(key <= qry, s, -jnp.inf)
        m_prev = m_scr[:, cols]
        m_new = jnp.maximum(m_prev, jnp.max(s, axis=0, keepdims=True))
        alpha = jnp.exp2(m_prev - m_new)
        p = jnp.exp2(s - m_new)
        acc_scr[:, cols] = alpha * acc_scr[:, cols] + jnp.dot(
            vt_scr[ki], p.astype(BF16), preferred_element_type=F32)
        m_scr[:, cols] = m_new

    ahead = ATTN_SCORES_AHEAD
    n_slots = s_scr.shape[0]
    assert n_chunks % n_slots == 0 and ahead < n_slots and ahead <= n_chunks
    first_diag = halves * qi
    items = [(d, c) for d in range(halves) for c in range(n_chunks) if c % halves >= d]
    assert items[:ahead] == [(0, c) for c in range(ahead)]

    for c in range(ahead):
        scores(0, c, c)

    def body(ki, carry):
        for c in range(n_chunks):
            nxt = c + ahead
            scores(ki + nxt // n_chunks, nxt % n_chunks, nxt % n_slots)
            softmax_pv(ki, c, c % n_slots, False)
        return carry

    def two_tiles(kp, carry):
        return body(2 * kp + 1, body(2 * kp, carry))

    assert halves % 2 == 0
    lax.fori_loop(0, first_diag // 2, two_tiles, 0)

    for n, (d, c) in enumerate(items):
        if n + ahead < len(items):
            d_next, c_next = items[n + ahead]
            scores(first_diag + d_next, c_next, (n + ahead) % n_slots)
        softmax_pv(first_diag + d, c, n % n_slots, c % halves == d)

    lq = lq_ref[...]
    lam = (jnp.exp(jnp.sum(lq[0:1] * lq[1:2], axis=-1, keepdims=True))
           - jnp.exp(jnp.sum(lq[2:3] * lq[3:4], axis=-1, keepdims=True)) + lambda_init)
    inv_l = 1.0 / acc_scr[LANES:LANES + 1, :]
    o = (acc_scr[:LANES, :tq] * inv_l[:, :tq] - lam * (acc_scr[:LANES, tq:] * inv_l[:, tq:]))
    o = o * lax.rsqrt(jnp.mean(o * o, axis=0, keepdims=True) + SUBLN_EPS) * sg_ref[...]
    o_ref[...] = (o * (1.0 - lambda_init)).T.astype(BF16)


def _alibi_key_columns(S):
    j = np.arange(S, dtype=np.float64)
    cols = np.zeros((DIFF_HEADS, S, LANES), np.float32)
    for h, slope in enumerate(_alibi_slopes(DIFF_HEADS)):
        rest = slope * LOG2E * j
        for part in range(POS_PARTS):
            piece = rest.astype(ml_dtypes.bfloat16).astype(np.float64)
            cols[h, :, part] = piece
            rest = rest - piece
    return jnp.asarray(cols, BF16)


def _attn_a(qv_t, k3, lam_qk, subln_g, lambda_init, B, S):
    T = B * S
    H = DIFF_HEADS
    tq = ATTN_QUERY_TILE
    tk = ATTN_KEY_TILE
    nq = S // tq
    kernel = functools.partial(_attn_a_kernel, lambda_init=lambda_init)
    return pl.pallas_call(
        kernel,
        grid=(B, H, nq),
        in_specs=[
            pl.BlockSpec((None, LANES, tq), lambda b, h, qi: (h, 0, b * nq + qi)),
            pl.BlockSpec((None, S, LANES), lambda b, h, qi: (h, b, 0)),
            pl.BlockSpec((None, LANES, S), lambda b, h, qi: (H + h, 0, b)),
            pl.BlockSpec((None, S, LANES), lambda b, h, qi: (h, 0, 0)),
            _resident((4, HEAD_DIM)),
            _resident((LANES, 1)),
        ],
        out_specs=pl.BlockSpec((tq, LANES), lambda b, h, qi: (b * nq + qi, h)),
        out_shape=jax.ShapeDtypeStruct((T, D_MODEL), BF16),
        scratch_shapes=[
            pltpu.VMEM((S, 2 * LANES), BF16),
            pltpu.VMEM((S // tk, LANES + SUM_ROWS, tk), BF16),
            pltpu.VMEM((2 * LANES, 2 * tq), BF16),
            pltpu.VMEM((ATTN_SCORE_SLOTS, tk, tk), F32),
            pltpu.VMEM((1, 2 * tq), F32),
            pltpu.VMEM((LANES + SUM_ROWS, 2 * tq), F32),
        ],
        compiler_params=_params(3),
        name="diff_attn",
    )(qv_t, k3, qv_t, _alibi_key_columns(S), lam_qk, subln_g.reshape(LANES, 1))


def _attn_b_kernel(sinks_ref, q_ref, k_ref, v_ref, bias_ref, o_ref, s_scr):
    dim = lax.broadcasted_iota(jnp.int32, (LANES, WINDOW), 0)
    pairs_per_kv = SWA_GROUP // 2
    group_lanes = SWA_GROUP * WINDOW
    ones_rows = jnp.ones((SUM_ROWS, 2 * WINDOW), BF16)

    def window_start(qb):
        nb = pl.program_id(1) * SWA_BLOCKS_PER_STEP + qb
        return nb, pl.multiple_of(jnp.maximum(nb - 1, 0) * WINDOW, WINDOW)

    def scores(qb, kh):
        _, start = window_start(qb)
        kwin = k_ref[kh, pl.ds(start, 2 * WINDOW), :]
        cols = []
        for cpair in range(pairs_per_kv):
            qt = q_ref[kh * pairs_per_kv + cpair, qb * WINDOW:(qb + 1) * WINDOW, :].T
            zero = jnp.zeros_like(qt)
            cols.append(jnp.where(dim < HEAD_DIM, qt, zero))
            cols.append(jnp.where(dim >= HEAD_DIM, qt, zero))
        return jnp.dot(kwin, jnp.concatenate(cols, axis=1),
                       preferred_element_type=F32)

    def softmax_pv(qb, kh, s_ref):
        nb, start = window_start(qb)
        bias = bias_ref.at[jnp.minimum(nb, 1) if qb == 0 else 1]
        vt = v_ref[pl.ds(start, 2 * WINDOW), :].T[kh * HEAD_DIM:(kh + 1) * HEAD_DIM]
        v_ext = jnp.concatenate([vt, ones_rows], axis=0)
        for cpair in range(pairs_per_kv):
            lanes = slice(2 * cpair * WINDOW, (2 * cpair + 2) * WINDOW)
            sink = jnp.concatenate(
                [jnp.full((1, WINDOW), sinks_ref[kh * SWA_GROUP + 2 * cpair + g] * LOG2E, F32)
                 for g in range(2)], axis=1)
            b0 = kh * group_lanes + 2 * cpair * WINDOW
            s = s_ref[:, lanes] + bias[:, b0:b0 + 2 * WINDOW]
            m = jnp.maximum(jnp.max(s, axis=0, keepdims=True), sink)
            p = jnp.exp2(s - m).astype(BF16)
            o = jnp.dot(v_ext, p, preferred_element_type=F32)
            denom = o[HEAD_DIM:HEAD_DIM + 1] + jnp.exp2(sink - m)
            o = o[:HEAD_DIM] * (1.0 / denom)
            pair = jnp.concatenate([o[:, :WINDOW], o[:, WINDOW:]], axis=0)
            col = (kh * pairs_per_kv + cpair) * LANES
            o_ref[qb * WINDOW:(qb + 1) * WINDOW, col:col + LANES] = pair.T.astype(BF16)

    items = [(qb, kh) for qb in range(SWA_BLOCKS_PER_STEP) for kh in range(SWA_KV_HEADS)]
    s_scr[0] = scores(*items[0])
    for n, item in enumerate(items):
        if n + 1 < len(items):
            s_scr[(n + 1) % 2] = scores(*items[n + 1])
        softmax_pv(*item, s_scr.at[n % 2])


def _swa_bias():
    key = np.arange(2 * WINDOW)[:, None]
    qry = np.arange(WINDOW)[None, :]
    bias = np.full((2, 2 * WINDOW, SWA_Q_HEADS * WINDOW), -np.inf, np.float32)
    for case, offset in enumerate((0, WINDOW)):
        dist = offset + qry - key
        valid = (dist >= 0) & (dist < WINDOW)
        for h, slope in enumerate(_alibi_slopes(SWA_Q_HEADS)):
            bias[case, :, h * WINDOW:(h + 1) * WINDOW] = np.where(valid, -slope * LOG2E * dist, -np.inf)
    return jnp.asarray(bias)


def _attn_b(q3, k3, v3, sinks, B, S):
    T = B * S
    rows = SWA_BLOCKS_PER_STEP * WINDOW
    nstep = S // rows
    n_pairs = SWA_Q_HEADS // 2
    return pl.pallas_call(
        _attn_b_kernel,
        grid_spec=pltpu.PrefetchScalarGridSpec(
            num_scalar_prefetch=1,
            grid=(B, nstep),
            in_specs=[
                pl.BlockSpec((n_pairs, rows, LANES), lambda b, n, *_: (0, b * nstep + n, 0)),
                pl.BlockSpec((SWA_KV_HEADS, S, LANES), lambda b, n, *_: (0, b, 0)),
                pl.BlockSpec((None, S, LANES), lambda b, n, *_: (SWA_KV_HEADS, b, 0)),
                _resident((2, 2 * WINDOW, SWA_Q_HEADS * WINDOW)),
            ],
            out_specs=pl.BlockSpec((rows, D_MODEL), lambda b, n, *_: (b * nstep + n, 0)),
            scratch_shapes=[pltpu.VMEM((2, 2 * WINDOW, SWA_GROUP * WINDOW), F32)],
        ),
        out_shape=jax.ShapeDtypeStruct((T, D_MODEL), BF16),
        compiler_params=_params(2),
        name="swa_attn",
    )(sinks, q3, k3, v3, _swa_bias())


def _gelu_tanh_of_half(h):
    c = math.sqrt(2.0 / math.pi)
    return h + h * jnp.tanh(h * (2.0 * c + (8.0 * 0.044715 * c) * (h * h)))


def _ffn_kernel(o_ref, x_ref, g1_ref, g2_ref, g3_ref, wo_ref, wgu_ref, wd_ref, cw_ref, cb_ref, out_ref,
                x1_scr, h_scr, a_scr, carry_scr, gate0_scr, gate1_scr, up0_scr, up1_scr,
                *, tiles_per_seq):
    i = pl.program_id(0)
    tm = x_ref.shape[0]
    rb = FF_ROW_BLOCK
    n_blocks = tm // rb
    tf = FF_CHUNK
    slots = ((gate0_scr, up0_scr), (gate1_scr, up1_scr))

    @pl.when(i == 0)
    def _():
        carry_scr[...] = jnp.zeros(carry_scr.shape, F32)

    seq_start = (i % tiles_per_seq) == 0

    def mix_norm(r):
        rows = slice(r * rb, (r + 1) * rb)
        mix = jnp.dot(o_ref[rows, :], wo_ref[...], preferred_element_type=F32)
        x1 = x_ref[rows, :] + _rms(mix, g1_ref[...], NORM_EPS)
        x1_scr[rows, :] = x1
        h_scr[rows, :] = _rms(x1, g2_ref[...], NORM_EPS).astype(BF16)

    def up_proj(r):
        gate_scr, up_scr = slots[r % 2]
        if r == 0:
            gate_scr[0:8, :] = jnp.where(seq_start, 0.0, carry_scr[...])
        else:
            gate_scr[0:8, :] = slots[(r - 1) % 2][0][rb:rb + 8, :]
        h = h_scr[r * rb:(r + 1) * rb, :]
        for c in range(0, D_FF, FF_DOT_COLS):
            w = min(FF_DOT_COLS, D_FF - c)
            gate_scr[8:8 + rb, c:c + w] = jnp.dot(h, wgu_ref[:, c:c + w], preferred_element_type=F32)
            up_scr[:, c:c + w] = jnp.dot(h, wgu_ref[:, D_FF + c:D_FF + c + w],
                                         preferred_element_type=F32)

    def act_down(r):
        gate_scr, up_scr = slots[r % 2]
        rows = slice(r * rb, (r + 1) * rb)
        for c in range(0, D_FF, tf):
            cols = slice(c, c + tf)
            cw = 0.5 * cw_ref[:, cols]
            half_conv = (cw[2:3] * gate_scr[8:8 + rb, cols] + 0.5 * cb_ref[:, cols]
                         + cw[1:2] * gate_scr[7:7 + rb, cols]
                         + cw[0:1] * gate_scr[6:6 + rb, cols])
            a_scr[:, cols] = (_gelu_tanh_of_half(half_conv) * up_scr[:, cols]).astype(BF16)
        f = jnp.dot(a_scr[...], wd_ref[...], preferred_element_type=F32)
        out_ref[rows, :] = x1_scr[rows, :] + _rms(f, g3_ref[...], NORM_EPS)
        if r == n_blocks - 1:
            carry_scr[...] = gate_scr[rb:rb + 8, :]

    for r in range(n_blocks):
        mix_norm(r)
    up_proj(0)
    for r in range(n_blocks):
        if r + 1 < n_blocks:
            up_proj(r + 1)
        act_down(r)


def _post_ffn(o, x, g1, g2, g3, w_o, w_up, conv_w, conv_b, w_down, S, in_place):
    T, D = x.shape
    tm = FF_ROW_TILE
    rb = FF_ROW_BLOCK
    kernel = functools.partial(_ffn_kernel, tiles_per_seq=S // tm)
    return pl.pallas_call(
        kernel,
        grid=(T // tm,),
        in_specs=[
            pl.BlockSpec((tm, D), lambda i: (i, 0)),
            pl.BlockSpec((tm, D), lambda i: (i, 0)),
            _resident((1, D)),
            _resident((1, D)),
            _resident((1, D)),
            _resident((D, D)),
            _resident((D, 2 * D_FF)),
            _resident((D_FF, D)),
            _resident((CONV_WIDTH, D_FF)),
            _resident((1, D_FF)),
        ],
        out_specs=pl.BlockSpec((tm, D), lambda i: (i, 0)),
        out_shape=jax.ShapeDtypeStruct((T, D), F32),
        scratch_shapes=[
            pltpu.VMEM((tm, D), F32),
            pltpu.VMEM((tm, D), BF16),
            pltpu.VMEM((rb, D_FF), BF16),
            pltpu.VMEM((8, D_FF), F32),
            pltpu.VMEM((rb + 8, D_FF), F32),
            pltpu.VMEM((rb + 8, D_FF), F32),
            pltpu.VMEM((rb, D_FF), F32),
            pltpu.VMEM((rb, D_FF), F32),
        ],
        input_output_aliases={1: 0} if in_place else {},
        compiler_params=_params(1),
        name="post_ffn",
    )(o, x, g1.reshape(1, D), g2.reshape(1, D), g3.reshape(1, D), w_o.astype(BF16),
      w_up.astype(BF16), w_down.astype(BF16), conv_w, conv_b.reshape(1, D_FF))


def _dup_heads(w, n_heads):
    lead = w.shape[:-1]
    w = w.reshape(lead + (n_heads, 1, HEAD_DIM))
    return jnp.broadcast_to(w, lead + (n_heads, 2, HEAD_DIM)).reshape(lead + (2 * n_heads * HEAD_DIM,))


def kernel(x, norm_gains, w_qkv_a, lambda_qk_a, subln_a, w_o_a, kv_norm, w_kv_b, b_kv_b, w_q_b, b_q_b,
           sinks_b, w_o_b, w_up, conv_w, conv_b, w_down):
    B, S, D = x.shape
    T = B * S
    assert D == D_MODEL and S % ATTN_QUERY_TILE == 0 and S % ROW_TILE == 0 and S % FF_ROW_TILE == 0 and S >= 2 * WINDOW
    xt = x.reshape(T, D)
    kv3 = None
    for layer in range(DEPTH):
        g = norm_gains[layer]
        if layer < N_A_LAYERS:
            lambda_init = 0.8 - 0.6 * math.exp(-0.3 * layer)
            w_q, w_k, w_v = jnp.split(w_qkv_a[layer].astype(BF16), 3, axis=1)
            qv_t, k3 = _proj(xt, [
                (g[0], jnp.concatenate([w_q, w_v], axis=1), jnp.zeros((2 * D,), F32),
                 DIFF_HEADS, QK_SCALE * LOG2E, True),
                (g[0], w_k, jnp.zeros((D,), F32), 0, 1.0, False)])
            o = _attn_a(qv_t, k3, lambda_qk_a[layer], subln_a[layer], lambda_init, B, S)
            w_o = w_o_a[layer]
        else:
            j = layer - N_A_LAYERS
            q_group = (g[0], w_q_b[j].astype(BF16), b_q_b[j], SWA_Q_HEADS // 2, QK_SCALE * LOG2E, False)
            if kv3 is None:
                kvw = SWA_KV_HEADS * HEAD_DIM
                w_kv = jnp.concatenate([_dup_heads(w_kv_b[:, :kvw], SWA_KV_HEADS), w_kv_b[:, kvw:]], axis=1)
                b_kv = jnp.concatenate([_dup_heads(b_kv_b[:kvw], SWA_KV_HEADS), b_kv_b[kvw:]])
                q3, kv3 = _proj(xt, [q_group, (kv_norm, w_kv.astype(BF16), b_kv, 0, 1.0, False)])
            else:
                q3, = _proj(xt, [q_group])
            o = _attn_b(q3, kv3, kv3, sinks_b[j], B, S)
            w_o = w_o_b[j]
        xt = _post_ffn(o, xt, g[1], g[2], g[3], w_o, w_up[layer], conv_w[layer], conv_b[layer],
                       w_down[layer], S, in_place=layer > 0)
    return xt.reshape(B, S, D)
```

```python
import functools
import math

import ml_dtypes
import numpy as np
import jax
import jax.numpy as jnp
from jax import lax
from jax.experimental import pallas as pl
from jax.experimental.pallas import tpu as pltpu

D_MODEL = 1024
HEAD_DIM = 64
DEPTH = 4
N_A_LAYERS = DEPTH // 2
DIFF_HEADS = D_MODEL // (2 * HEAD_DIM)
SWA_Q_HEADS = D_MODEL // HEAD_DIM
SWA_KV_HEADS = 2
SWA_GROUP = SWA_Q_HEADS // SWA_KV_HEADS
WINDOW = 128
D_FF = 2816
CONV_WIDTH = 3
NORM_EPS = 1e-6
SUBLN_EPS = 1e-5

LANES = 128
QK_SCALE = HEAD_DIM ** -0.5
VMEM_LIMIT_BYTES = 56 * 1024 * 1024
PROJ_DOT_SLABS = 4
ROW_TILE = 1024
ATTN_QUERY_TILE = 2048
ATTN_KEY_TILE = 512
ATTN_SCORES_AHEAD = 1
ATTN_SCORE_SLOTS = 2
SUM_ROWS = 16
SWA_BLOCKS_PER_STEP = 16
FF_CHUNK = 256
FF_DOT_COLS = 256
FF_ROW_TILE = 1024
FF_ROW_BLOCK = 256
POS_PARTS = 3
LOG2E = math.log2(math.e)

BF16 = jnp.bfloat16
F32 = jnp.float32


def _alibi_slopes(n):
    return [2.0 ** (-8.0 * (i + 1) / n) for i in range(n)]


def _rms(x, g, eps):
    return x * lax.rsqrt(jnp.mean(x * x, axis=-1, keepdims=True) + eps) * g


def _params(n_axes):
    return pltpu.CompilerParams(
        dimension_semantics=("arbitrary",) * n_axes, vmem_limit_bytes=VMEM_LIMIT_BYTES)


def _resident(shape):
    nd = len(shape)
    return pl.BlockSpec(shape, lambda *_: (0,) * nd, pipeline_mode=pl.Buffered(1))


def _proj_kernel(x_ref, *refs, scaled):
    n = len(scaled)
    x = x_ref[...]
    xn = x * lax.rsqrt(jnp.mean(x * x, axis=-1, keepdims=True) + NORM_EPS)
    for k, (n_scaled, scale, transposed) in enumerate(scaled):
        g_ref, w_ref, b_ref = refs[3 * k:3 * k + 3]
        o_ref = refs[3 * n + k]
        n_slabs = o_ref.shape[0]
        h = (xn * g_ref[...]).astype(BF16)
        for c0 in range(0, n_slabs, PROJ_DOT_SLABS):
            nc = min(PROJ_DOT_SLABS, n_slabs - c0)
            cols = slice(c0 * LANES, (c0 + nc) * LANES)
            y = jnp.dot(h, w_ref[:, cols], preferred_element_type=F32) + b_ref[:, cols]
            for c in range(nc):
                slab = y[:, c * LANES:(c + 1) * LANES]
                if c0 + c < n_scaled:
                    slab = slab * scale
                o_ref[c0 + c] = (slab.T if transposed else slab).astype(BF16)


def _proj(x, groups):
    T, D = x.shape
    tm = ROW_TILE
    operands, in_specs, out_specs, out_shapes = [x], [pl.BlockSpec((tm, D), lambda i: (i, 0))], [], []
    for g, w, b, _, _, transposed in groups:
        N = w.shape[1]
        operands += [g.reshape(1, D), w, b.reshape(1, N)]
        in_specs += [_resident((1, D)), _resident((D, N)), _resident((1, N))]
        if transposed:
            out_specs.append(pl.BlockSpec((N // LANES, LANES, tm), lambda i: (0, 0, i)))
            out_shapes.append(jax.ShapeDtypeStruct((N // LANES, LANES, T), BF16))
        else:
            out_specs.append(pl.BlockSpec((N // LANES, tm, LANES), lambda i: (0, i, 0)))
            out_shapes.append(jax.ShapeDtypeStruct((N // LANES, T, LANES), BF16))
    return pl.pallas_call(
        functools.partial(_proj_kernel, scaled=tuple(g[3:] for g in groups)),
        grid=(T // tm,),
        in_specs=in_specs,
        out_specs=out_specs,
        out_shape=out_shapes,
        compiler_params=_params(1),
        name="norm_proj",
    )(*operands)


def _attn_a_kernel(q_ref, k_ref, v_ref, kpos_ref, lq_ref, sg_ref, o_ref,
                   kk_scr, vt_scr, qq_scr, s_scr, m_scr, acc_scr, *, lambda_init):
    qi = pl.program_id(2)
    tq = q_ref.shape[1]
    tk = ATTN_KEY_TILE
    halves = tq // tk
    n_chunks = 2 * halves
    assert n_chunks % 2 == 0

    @pl.when(qi == 0)
    def _():
        kk_scr[:, :LANES] = k_ref[...]
        kk_scr[:, LANES:] = kpos_ref[...]
        for t in range(vt_scr.shape[0]):
            vt_scr[t, :LANES, :] = v_ref[:, t * tk:(t + 1) * tk]
            vt_scr[t, LANES:, :] = jnp.ones((SUM_ROWS, tk), BF16)

    q = q_ref[...]
    dim = lax.broadcasted_iota(jnp.int32, (LANES, tq), 0)
    qq_scr[:LANES, :tq] = jnp.where(dim < HEAD_DIM, q, jnp.zeros_like(q))
    qq_scr[:LANES, tq:] = jnp.where(dim >= HEAD_DIM, q, jnp.zeros_like(q))
    row = lax.broadcasted_iota(jnp.int32, (LANES, 2 * tq), 0)
    qq_scr[LANES:, :] = jnp.where(row < POS_PARTS, 1.0, 0.0).astype(BF16)

    m_scr[...] = jnp.full(m_scr.shape, -jnp.inf, F32)
    acc_scr[...] = jnp.zeros(acc_scr.shape, F32)

    def scores(ki, c, slot):
        start = pl.multiple_of(ki * tk, tk)
        s_scr[slot] = jnp.dot(kk_scr[pl.ds(start, tk), :], qq_scr[:, c * tk:(c + 1) * tk],
                              preferred_element_type=F32)

    def softmax_pv(ki, c, slot, triangular):
        cols = slice(c * tk, (c + 1) * tk)
        s = s_scr[slot]
        if triangular:
            key = lax.broadcasted_iota(jnp.int32, (tk, tk), 0)
            qry = lax.broadcasted_iota(jnp.int32, (tk, tk), 1)
            s = jnp.where(key <= qry, s, -jnp.inf)
        m_prev = m_scr[:, cols]
        m_new = jnp.maximum(m_prev, jnp.max(s, axis=0, keepdims=True))
        alpha = jnp.exp2(m_prev - m_new)
        p = jnp.exp2(s - m_new)
        acc_scr[:, cols] = alpha * acc_scr[:, cols] + jnp.dot(
            vt_scr[ki], p.astype(BF16), preferred_element_type=F32)
        m_scr[:, cols] = m_new

    ahead = ATTN_SCORES_AHEAD
    n_slots = s_scr.shape[0]
    assert n_chunks % n_slots == 0 and ahead < n_slots and ahead <= n_chunks
    first_diag = halves * qi
    items = [(d, c) for d in range(halves) for c in range(n_chunks) if c % halves >= d]
    assert items[:ahead] == [(0, c) for c in range(ahead)]

    for c in range(ahead):
        scores(0, c, c)

    def body(ki, carry):
        for c in range(n_chunks):
            nxt = c + ahead
            scores(ki + nxt // n_chunks, nxt % n_chunks, nxt % n_slots)
            softmax_pv(ki, c, c % n_slots, False)
        return carry

    def tile_group(kg, carry):
        for t in range(halves):
            carry = body(halves * kg + t, carry)
        return carry

    lax.fori_loop(0, qi, tile_group, 0)

    for n, (d, c) in enumerate(items):
        if n + ahead < len(items):
            d_next, c_next = items[n + ahead]
            scores(first_diag + d_next, c_next, (n + ahead) % n_slots)
        softmax_pv(first_diag + d, c, n % n_slots, c % halves == d)

    lq = lq_ref[...]
    lam = (jnp.exp(jnp.sum(lq[0:1] * lq[1:2], axis=-1, keepdims=True))
           - jnp.exp(jnp.sum(lq[2:3] * lq[3:4], axis=-1, keepdims=True)) + lambda_init)
    inv_l = 1.0 / acc_scr[LANES:LANES + 1, :]
    o = (acc_scr[:LANES, :tq] * inv_l[:, :tq] - lam * (acc_scr[:LANES, tq:] * inv_l[:, tq:]))
    o = o * lax.rsqrt(jnp.mean(o * o, axis=0, keepdims=True) + SUBLN_EPS) * sg_ref[...]
    o_ref[...] = (o * (1.0 - lambda_init)).T.astype(BF16)


def _alibi_key_columns(S):
    j = np.arange(S, dtype=np.float64)
    cols = np.zeros((DIFF_HEADS, S, LANES), np.float32)
    for h, slope in enumerate(_alibi_slopes(DIFF_HEADS)):
        rest = slope * LOG2E * j
        for part in range(POS_PARTS):
            piece = rest.astype(ml_dtypes.bfloat16).astype(np.float64)
            cols[h, :, part] = piece
            rest = rest - piece
    return jnp.asarray(cols, BF16)


def _attn_a(qv_t, k3, lam_qk, subln_g, lambda_init, B, S):
    T = B * S
    H = DIFF_HEADS
    tq = ATTN_QUERY_TILE
    tk = ATTN_KEY_TILE
    nq = S // tq
    kernel = functools.partial(_attn_a_kernel, lambda_init=lambda_init)
    return pl.pallas_call(
        kernel,
        grid=(B, H, nq),
        in_specs=[
            pl.BlockSpec((None, LANES, tq), lambda b, h, qi: (h, 0, b * nq + qi)),
            pl.BlockSpec((None, S, LANES), lambda b, h, qi: (h, b, 0)),
            pl.BlockSpec((None, LANES, S), lambda b, h, qi: (H + h, 0, b)),
            pl.BlockSpec((None, S, LANES), lambda b, h, qi: (h, 0, 0)),
            _resident((4, HEAD_DIM)),
            _resident((LANES, 1)),
        ],
        out_specs=pl.BlockSpec((tq, LANES), lambda b, h, qi: (b * nq + qi, h)),
        out_shape=jax.ShapeDtypeStruct((T, D_MODEL), BF16),
        scratch_shapes=[
            pltpu.VMEM((S, 2 * LANES), BF16),
            pltpu.VMEM((S // tk, LANES + SUM_ROWS, tk), BF16),
            pltpu.VMEM((2 * LANES, 2 * tq), BF16),
            pltpu.VMEM((ATTN_SCORE_SLOTS, tk, tk), F32),
            pltpu.VMEM((1, 2 * tq), F32),
            pltpu.VMEM((LANES + SUM_ROWS, 2 * tq), F32),
        ],
        compiler_params=_params(3),
        name="diff_attn",
    )(qv_t, k3, qv_t, _alibi_key_columns(S), lam_qk, subln_g.reshape(LANES, 1))


def _attn_b_kernel(sinks_ref, q_ref, k_ref, v_ref, bias_ref, o_ref, s_scr):
    dim = lax.broadcasted_iota(jnp.int32, (LANES, WINDOW), 0)
    pairs_per_kv = SWA_GROUP // 2
    group_lanes = SWA_GROUP * WINDOW
    ones_rows = jnp.ones((SUM_ROWS, 2 * WINDOW), BF16)

    def window_start(qb):
        nb = pl.program_id(1) * SWA_BLOCKS_PER_STEP + qb
        return nb, pl.multiple_of(jnp.maximum(nb - 1, 0) * WINDOW, WINDOW)

    def scores(qb, kh):
        _, start = window_start(qb)
        kwin = k_ref[kh, pl.ds(start, 2 * WINDOW), :]
        cols = []
        for cpair in range(pairs_per_kv):
            qt = q_ref[kh * pairs_per_kv + cpair, qb * WINDOW:(qb + 1) * WINDOW, :].T
            zero = jnp.zeros_like(qt)
            cols.append(jnp.where(dim < HEAD_DIM, qt, zero))
            cols.append(jnp.where(dim >= HEAD_DIM, qt, zero))
        return jnp.dot(kwin, jnp.concatenate(cols, axis=1),
                       preferred_element_type=F32)

    def softmax_pv(qb, kh, s_ref):
        nb, start = window_start(qb)
        bias = bias_ref.at[jnp.minimum(nb, 1) if qb == 0 else 1]
        vt = v_ref[pl.ds(start, 2 * WINDOW), :].T[kh * HEAD_DIM:(kh + 1) * HEAD_DIM]
        v_ext = jnp.concatenate([vt, ones_rows], axis=0)
        for cpair in range(pairs_per_kv):
            lanes = slice(2 * cpair * WINDOW, (2 * cpair + 2) * WINDOW)
            sink = jnp.concatenate(
                [jnp.full((1, WINDOW), sinks_ref[kh * SWA_GROUP + 2 * cpair + g] * LOG2E, F32)
                 for g in range(2)], axis=1)
            b0 = kh * group_lanes + 2 * cpair * WINDOW
            s = s_ref[:, lanes] + bias[:, b0:b0 + 2 * WINDOW]
            m = jnp.maximum(jnp.max(s, axis=0, keepdims=True), sink)
            p = jnp.exp2(s - m).astype(BF16)
            o = jnp.dot(v_ext, p, preferred_element_type=F32)
            denom = o[HEAD_DIM:HEAD_DIM + 1] + jnp.exp2(sink - m)
            o = o[:HEAD_DIM] * (1.0 / denom)
            pair = jnp.concatenate([o[:, :WINDOW], o[:, WINDOW:]], axis=0)
            col = (kh * pairs_per_kv + cpair) * LANES
            o_ref[qb * WINDOW:(qb + 1) * WINDOW, col:col + LANES] = pair.T.astype(BF16)

    items = [(qb, kh) for qb in range(SWA_BLOCKS_PER_STEP) for kh in range(SWA_KV_HEADS)]
    s_scr[0] = scores(*items[0])
    for n, item in enumerate(items):
        if n + 1 < len(items):
            s_scr[(n + 1) % 2] = scores(*items[n + 1])
        softmax_pv(*item, s_scr.at[n % 2])


def _swa_bias():
    key = np.arange(2 * WINDOW)[:, None]
    qry = np.arange(WINDOW)[None, :]
    bias = np.full((2, 2 * WINDOW, SWA_Q_HEADS * WINDOW), -np.inf, np.float32)
    for case, offset in enumerate((0, WINDOW)):
        dist = offset + qry - key
        valid = (dist >= 0) & (dist < WINDOW)
        for h, slope in enumerate(_alibi_slopes(SWA_Q_HEADS)):
            bias[case, :, h * WINDOW:(h + 1) * WINDOW] = np.where(valid, -slope * LOG2E * dist, -np.inf)
    return jnp.asarray(bias)


def _attn_b(q3, k3, v3, sinks, B, S):
    T = B * S
    rows = SWA_BLOCKS_PER_STEP * WINDOW
    nstep = S // rows
    n_pairs = SWA_Q_HEADS // 2
    return pl.pallas_call(
        _attn_b_kernel,
        grid_spec=pltpu.PrefetchScalarGridSpec(
            num_scalar_prefetch=1,
            grid=(B, nstep),
            in_specs=[
                pl.BlockSpec((n_pairs, rows, LANES), lambda b, n, *_: (0, b * nstep + n, 0)),
                pl.BlockSpec((SWA_KV_HEADS, S, LANES), lambda b, n, *_: (0, b, 0)),
                pl.BlockSpec((None, S, LANES), lambda b, n, *_: (SWA_KV_HEADS, b, 0)),
                _resident((2, 2 * WINDOW, SWA_Q_HEADS * WINDOW)),
            ],
            out_specs=pl.BlockSpec((rows, D_MODEL), lambda b, n, *_: (b * nstep + n, 0)),
            scratch_shapes=[pltpu.VMEM((2, 2 * WINDOW, SWA_GROUP * WINDOW), F32)],
        ),
        out_shape=jax.ShapeDtypeStruct((T, D_MODEL), BF16),
        compiler_params=_params(2),
        name="swa_attn",
    )(sinks, q3, k3, v3, _swa_bias())


def _gelu_tanh_of_half(h):
    c = math.sqrt(2.0 / math.pi)
    return h + h * jnp.tanh(h * (2.0 * c + (8.0 * 0.044715 * c) * (h * h)))


def _ffn_kernel(o_ref, x_ref, g1_ref, g2_ref, g3_ref, wo_ref, wgu_ref, wd_ref, cw_ref, cb_ref, out_ref,
                x1_scr, h_scr, a_scr, carry_scr, gate0_scr, gate1_scr, up0_scr, up1_scr,
                *, tiles_per_seq):
    i = pl.program_id(0)
    tm = x_ref.shape[0]
    rb = FF_ROW_BLOCK
    n_blocks = tm // rb
    tf = FF_CHUNK
    slots = ((gate0_scr, up0_scr), (gate1_scr, up1_scr))

    @pl.when(i == 0)
    def _():
        carry_scr[...] = jnp.zeros(carry_scr.shape, F32)

    seq_start = (i % tiles_per_seq) == 0

    def mix_norm(r):
        rows = slice(r * rb, (r + 1) * rb)
        mix = jnp.dot(o_ref[rows, :], wo_ref[...], preferred_element_type=F32)
        x1 = x_ref[rows, :] + _rms(mix, g1_ref[...], NORM_EPS)
        x1_scr[rows, :] = x1
        h_scr[rows, :] = _rms(x1, g2_ref[...], NORM_EPS).astype(BF16)

    def up_proj(r):
        gate_scr, up_scr = slots[r % 2]
        if r == 0:
            gate_scr[0:8, :] = jnp.where(seq_start, 0.0, carry_scr[...])
        else:
            gate_scr[0:8, :] = slots[(r - 1) % 2][0][rb:rb + 8, :]
        h = h_scr[r * rb:(r + 1) * rb, :]
        for c in range(0, D_FF, FF_DOT_COLS):
            w = min(FF_DOT_COLS, D_FF - c)
            gate_scr[8:8 + rb, c:c + w] = jnp.dot(h, wgu_ref[:, c:c + w], preferred_element_type=F32)
            up_scr[:, c:c + w] = jnp.dot(h, wgu_ref[:, D_FF + c:D_FF + c + w],
                                         preferred_element_type=F32)

    def act_down(r):
        gate_scr, up_scr = slots[r % 2]
        rows = slice(r * rb, (r + 1) * rb)
        for c in range(0, D_FF, tf):
            cols = slice(c, c + tf)
            cw = 0.5 * cw_ref[:, cols]
            half_conv = (cw[2:3] * gate_scr[8:8 + rb, cols] + 0.5 * cb_ref[:, cols]
                         + cw[1:2] * gate_scr[7:7 + rb, cols]
                         + cw[0:1] * gate_scr[6:6 + rb, cols])
            a_scr[:, cols] = (_gelu_tanh_of_half(half_conv) * up_scr[:, cols]).astype(BF16)
        f = jnp.dot(a_scr[...], wd_ref[...], preferred_element_type=F32)
        out_ref[rows, :] = x1_scr[rows, :] + _rms(f, g3_ref[...], NORM_EPS)
        if r == n_blocks - 1:
            carry_scr[...] = gate_scr[rb:rb + 8, :]

    for r in range(n_blocks):
        mix_norm(r)
    up_proj(0)
    for r in range(n_blocks):
        if r + 1 < n_blocks:
            up_proj(r + 1)
        act_down(r)


def _post_ffn(o, x, g1, g2, g3, w_o, w_up, conv_w, conv_b, w_down, S, in_place):
    T, D = x.shape
    tm = FF_ROW_TILE
    rb = FF_ROW_BLOCK
    kernel = functools.partial(_ffn_kernel, tiles_per_seq=S // tm)
    return pl.pallas_call(
        kernel,
        grid=(T // tm,),
        in_specs=[
            pl.BlockSpec((tm, D), lambda i: (i, 0)),
            pl.BlockSpec((tm, D), lambda i: (i, 0)),
            _resident((1, D)),
            _resident((1, D)),
            _resident((1, D)),
            _resident((D, D)),
            _resident((D, 2 * D_FF)),
            _resident((D_FF, D)),
            _resident((CONV_WIDTH, D_FF)),
            _resident((1, D_FF)),
        ],
        out_specs=pl.BlockSpec((tm, D), lambda i: (i, 0)),
        out_shape=jax.ShapeDtypeStruct((T, D), F32),
        scratch_shapes=[
            pltpu.VMEM((tm, D), F32),
            pltpu.VMEM((tm, D), BF16),
            pltpu.VMEM((rb, D_FF), BF16),
            pltpu.VMEM((8, D_FF), F32),
            pltpu.VMEM((rb + 8, D_FF), F32),
            pltpu.VMEM((rb + 8, D_FF), F32),
            pltpu.VMEM((rb, D_FF), F32),
            pltpu.VMEM((rb, D_FF), F32),
        ],
        input_output_aliases={1: 0} if in_place else {},
        compiler_params=_params(1),
        name="post_ffn",
    )(o, x, g1.reshape(1, D), g2.reshape(1, D), g3.reshape(1, D), w_o.astype(BF16),
      w_up.astype(BF16), w_down.astype(BF16), conv_w, conv_b.reshape(1, D_FF))


def _dup_heads(w, n_heads):
    lead = w.shape[:-1]
    w = w.reshape(lead + (n_heads, 1, HEAD_DIM))
    return jnp.broadcast_to(w, lead + (n_heads, 2, HEAD_DIM)).reshape(lead + (2 * n_heads * HEAD_DIM,))


def kernel(x, norm_gains, w_qkv_a, lambda_qk_a, subln_a, w_o_a, kv_norm, w_kv_b, b_kv_b, w_q_b, b_q_b,
           sinks_b, w_o_b, w_up, conv_w, conv_b, w_down):
    B, S, D = x.shape
    T = B * S
    assert D == D_MODEL and S % ATTN_QUERY_TILE == 0 and S % ROW_TILE == 0 and S % FF_ROW_TILE == 0 and S >= 2 * WINDOW
    xt = x.reshape(T, D)
    kv3 = None
    for layer in range(DEPTH):
        g = norm_gains[layer]
        if layer < N_A_LAYERS:
            lambda_init = 0.8 - 0.6 * math.exp(-0.3 * layer)
            w_q, w_k, w_v = jnp.split(w_qkv_a[layer].astype(BF16), 3, axis=1)
            qv_t, k3 = _proj(xt, [
                (g[0], jnp.concatenate([w_q, w_v], axis=1), jnp.zeros((2 * D,), F32),
                 DIFF_HEADS, QK_SCALE * LOG2E, True),
                (g[0], w_k, jnp.zeros((D,), F32), 0, 1.0, False)])
            o = _attn_a(qv_t, k3, lambda_qk_a[layer], subln_a[layer], lambda_init, B, S)
            w_o = w_o_a[layer]
        else:
            j = layer - N_A_LAYERS
            q_group = (g[0], w_q_b[j].astype(BF16), b_q_b[j], SWA_Q_HEADS // 2, QK_SCALE * LOG2E, False)
            if kv3 is None:
                kvw = SWA_KV_HEADS * HEAD_DIM
                w_kv = jnp.concatenate([_dup_heads(w_kv_b[:, :kvw], SWA_KV_HEADS), w_kv_b[:, kvw:]], axis=1)
                b_kv = jnp.concatenate([_dup_heads(b_kv_b[:kvw], SWA_KV_HEADS), b_kv_b[kvw:]])
                q3, kv3 = _proj(xt, [q_group, (kv_norm, w_kv.astype(BF16), b_kv, 0, 1.0, False)])
            else:
                q3, = _proj(xt, [q_group])
            o = _attn_b(q3, kv3, kv3, sinks_b[j], B, S)
            w_o = w_o_b[j]
        xt = _post_ffn(o, xt, g[1], g[2], g[3], w_o, w_up[layer], conv_w[layer], conv_b[layer],
                       w_down[layer], S, in_place=layer > 0)
    return xt.reshape(B, S, D)
```

```python
import functools
import math

import ml_dtypes
import numpy as np
import jax
import jax.numpy as jnp
from jax import lax
from jax.experimental import pallas as pl
from jax.experimental.pallas import tpu as pltpu

D_MODEL = 1024
HEAD_DIM = 64
DEPTH = 4
N_A_LAYERS = DEPTH // 2
DIFF_HEADS = D_MODEL // (2 * HEAD_DIM)
SWA_Q_HEADS = D_MODEL // HEAD_DIM
SWA_KV_HEADS = 2
SWA_GROUP = SWA_Q_HEADS // SWA_KV_HEADS
WINDOW = 128
D_FF = 2816
CONV_WIDTH = 3
NORM_EPS = 1e-6
SUBLN_EPS = 1e-5

LANES = 128
QK_SCALE = HEAD_DIM ** -0.5
VMEM_LIMIT_BYTES = 56 * 1024 * 1024
PROJ_DOT_SLABS = 4
ROW_TILE = 1024
ATTN_QUERY_TILE = 2048
ATTN_KEY_TILE = 512
ATTN_SCORES_AHEAD = 1
ATTN_SCORE_SLOTS = 2
SUM_ROWS = 16
SWA_BLOCKS_PER_STEP = 16
FF_CHUNK = 256
FF_DOT_COLS = 256
FF_ROW_TILE = 1024
FF_ROW_BLOCK = 256
POS_PARTS = 3
LOG2E = math.log2(math.e)

BF16 = jnp.bfloat16
F32 = jnp.float32


def _alibi_slopes(n):
    return [2.0 ** (-8.0 * (i + 1) / n) for i in range(n)]


def _rms(x, g, eps):
    return x * lax.rsqrt(jnp.mean(x * x, axis=-1, keepdims=True) + eps) * g


def _params(n_axes):
    return pltpu.CompilerParams(
        dimension_semantics=("arbitrary",) * n_axes, vmem_limit_bytes=VMEM_LIMIT_BYTES)


def _resident(shape):
    nd = len(shape)
    return pl.BlockSpec(shape, lambda *_: (0,) * nd, pipeline_mode=pl.Buffered(1))


def _proj_kernel(x_ref, *refs, scaled):
    n = len(scaled)
    x = x_ref[...]
    xn = x * lax.rsqrt(jnp.mean(x * x, axis=-1, keepdims=True) + NORM_EPS)
    for k, (n_scaled, scale, transposed) in enumerate(scaled):
        g_ref, w_ref, b_ref = refs[3 * k:3 * k + 3]
        o_ref = refs[3 * n + k]
        n_slabs = o_ref.shape[0]
        h = (xn * g_ref[...]).astype(BF16)
        for c0 in range(0, n_slabs, PROJ_DOT_SLABS):
            nc = min(PROJ_DOT_SLABS, n_slabs - c0)
            cols = slice(c0 * LANES, (c0 + nc) * LANES)
            y = jnp.dot(h, w_ref[:, cols], preferred_element_type=F32) + b_ref[:, cols]
            for c in range(nc):
                slab = y[:, c * LANES:(c + 1) * LANES]
                if c0 + c < n_scaled:
                    slab = slab * scale
                o_ref[c0 + c] = (slab.T if transposed else slab).astype(BF16)


def _proj(x, groups):
    T, D = x.shape
    tm = ROW_TILE
    operands, in_specs, out_specs, out_shapes = [x], [pl.BlockSpec((tm, D), lambda i: (i, 0))], [], []
    for g, w, b, _, _, transposed in groups:
        N = w.shape[1]
        operands += [g.reshape(1, D), w, b.reshape(1, N)]
        in_specs += [_resident((1, D)), _resident((D, N)), _resident((1, N))]
        if transposed:
            out_specs.append(pl.BlockSpec((N // LANES, LANES, tm), lambda i: (0, 0, i)))
            out_shapes.append(jax.ShapeDtypeStruct((N // LANES, LANES, T), BF16))
        else:
            out_specs.append(pl.BlockSpec((N // LANES, tm, LANES), lambda i: (0, i, 0)))
            out_shapes.append(jax.ShapeDtypeStruct((N // LANES, T, LANES), BF16))
    return pl.pallas_call(
        functools.partial(_proj_kernel, scaled=tuple(g[3:] for g in groups)),
        grid=(T // tm,),
        in_specs=in_specs,
        out_specs=out_specs,
        out_shape=out_shapes,
        compiler_params=_params(1),
        name="norm_proj",
    )(*operands)


def _attn_a_kernel(q_ref, k_ref, v_ref, kpos_ref, lq_ref, sg_ref, o_ref,
                   kk_scr, vt_scr, qq_scr, s_scr, m_scr, acc_scr, *, lambda_init):
    qi = pl.program_id(2)
    tq = q_ref.shape[1]
    tk = ATTN_KEY_TILE
    halves = tq // tk
    n_chunks = 2 * halves
    assert n_chunks % 2 == 0

    @pl.when(qi == 0)
    def _():
        kk_scr[:, :LANES] = k_ref[...]
        kk_scr[:, LANES:] = kpos_ref[...]
        for t in range(vt_scr.shape[0]):
            vt_scr[t, :LANES, :] = v_ref[:, t * tk:(t + 1) * tk]
            vt_scr[t, LANES:, :] = jnp.ones((SUM_ROWS, tk), BF16)

    q = q_ref[...]
    dim = lax.broadcasted_iota(jnp.int32, (LANES, tq), 0)
    qq_scr[:LANES, :tq] = jnp.where(dim < HEAD_DIM, q, jnp.zeros_like(q))
    qq_scr[:LANES, tq:] = jnp.where(dim >= HEAD_DIM, q, jnp.zeros_like(q))
    row = lax.broadcasted_iota(jnp.int32, (LANES, 2 * tq), 0)
    qq_scr[LANES:, :] = jnp.where(row < POS_PARTS, 1.0, 0.0).astype(BF16)

    m_scr[...] = jnp.full(m_scr.shape, -jnp.inf, F32)
    acc_scr[...] = jnp.zeros(acc_scr.shape, F32)

    def scores(ki, c, slot):
        start = pl.multiple_of(ki * tk, tk)
        s_scr[slot] = jnp.dot(kk_scr[pl.ds(start, tk), :], qq_scr[:, c * tk:(c + 1) * tk],
                              preferred_element_type=F32)

    def softmax_pv(ki, c, slot, triangular):
        cols = slice(c * tk, (c + 1) * tk)
        s = s_scr[slot]
        if triangular:
            key = lax.broadcasted_iota(jnp.int32, (tk, tk), 0)
            qry = lax.broadcasted_iota(jnp.int32, (tk, tk), 1)
            s = jnp.where(key <= qry, s, -jnp.inf)
        m_prev = m_scr[:, cols]
        m_new = jnp.maximum(m_prev, jnp.max(s, axis=0, keepdims=True))
        alpha = jnp.exp2(m_prev - m_new)
        p = jnp.exp2(s - m_new)
        acc_scr[:, cols] = alpha * acc_scr[:, cols] + jnp.dot(
            vt_scr[ki], p.astype(BF16), preferred_element_type=F32)
        m_scr[:, cols] = m_new

    ahead = ATTN_SCORES_AHEAD
    n_slots = s_scr.shape[0]
    assert n_chunks % n_slots == 0 and ahead < n_slots and ahead <= n_chunks
    first_diag = halves * qi
    items = [(d, c) for d in range(halves) for c in range(n_chunks) if c % halves >= d]
    assert items[:ahead] == [(0, c) for c in range(ahead)]

    for c in range(ahead):
        scores(0, c, c)

    def body(ki, carry):
        for c in range(n_chunks):
            nxt = c + ahead
            scores(ki + nxt // n_chunks, nxt % n_chunks, nxt % n_slots)
            softmax_pv(ki, c, c % n_slots, False)
        return carry

    def tile_group(kg, carry):
        for t in range(halves):
            carry = body(halves * kg + t, carry)
        return carry

    lax.fori_loop(0, qi, tile_group, 0)

    for n, (d, c) in enumerate(items):
        if n + ahead < len(items):
            d_next, c_next = items[n + ahead]
            scores(first_diag + d_next, c_next, (n + ahead) % n_slots)
        softmax_pv(first_diag + d, c, n % n_slots, c % halves == d)

    lq = lq_ref[...]
    lam = (jnp.exp(jnp.sum(lq[0:1] * lq[1:2], axis=-1, keepdims=True))
           - jnp.exp(jnp.sum(lq[2:3] * lq[3:4], axis=-1, keepdims=True)) + lambda_init)
    inv_l = 1.0 / acc_scr[LANES:LANES + 1, :]
    o = (acc_scr[:LANES, :tq] * inv_l[:, :tq] - lam * (acc_scr[:LANES, tq:] * inv_l[:, tq:]))
    o = o * lax.rsqrt(jnp.mean(o * o, axis=0, keepdims=True) + SUBLN_EPS) * sg_ref[...]
    o_ref[...] = (o * (1.0 - lambda_init)).T.astype(BF16)


def _alibi_key_columns(S):
    j = np.arange(S, dtype=np.float64)
    cols = np.zeros((DIFF_HEADS, S, LANES), np.float32)
    for h, slope in enumerate(_alibi_slopes(DIFF_HEADS)):
        rest = slope * LOG2E * j
        for part in range(POS_PARTS):
            piece = rest.astype(ml_dtypes.bfloat16).astype(np.float64)
            cols[h, :, part] = piece
            rest = rest - piece
    return jnp.asarray(cols, BF16)


def _attn_a(qv_t, k3, lam_qk, subln_g, lambda_init, B, S):
    T = B * S
    H = DIFF_HEADS
    tq = ATTN_QUERY_TILE
    tk = ATTN_KEY_TILE
    nq = S // tq
    kernel = functools.partial(_attn_a_kernel, lambda_init=lambda_init)
    return pl.pallas_call(
        kernel,
        grid=(B, H, nq),
        in_specs=[
            pl.BlockSpec((None, LANES, tq), lambda b, h, qi: (h, 0, b * nq + qi)),
            pl.BlockSpec((None, S, LANES), lambda b, h, qi: (h, b, 0)),
            pl.BlockSpec((None, LANES, S), lambda b, h, qi: (H + h, 0, b)),
            pl.BlockSpec((None, S, LANES), lambda b, h, qi: (h, 0, 0)),
            _resident((4, HEAD_DIM)),
            _resident((LANES, 1)),
        ],
        out_specs=pl.BlockSpec((tq, LANES), lambda b, h, qi: (b * nq + qi, h)),
        out_shape=jax.ShapeDtypeStruct((T, D_MODEL), BF16),
        scratch_shapes=[
            pltpu.VMEM((S, 2 * LANES), BF16),
            pltpu.VMEM((S // tk, LANES + SUM_ROWS, tk), BF16),
            pltpu.VMEM((2 * LANES, 2 * tq), BF16),
            pltpu.VMEM((ATTN_SCORE_SLOTS, tk, tk), F32),
            pltpu.VMEM((1, 2 * tq), F32),
            pltpu.VMEM((LANES + SUM_ROWS, 2 * tq), F32),
        ],
        compiler_params=_params(3),
        name="diff_attn",
    )(qv_t, k3, qv_t, _alibi_key_columns(S), lam_qk, subln_g.reshape(LANES, 1))


def _attn_b_kernel(sinks_ref, q_ref, k_ref, v_ref, bias_ref, o_ref, s_scr):
    dim = lax.broadcasted_iota(jnp.int32, (LANES, WINDOW), 0)
    pairs_per_kv = SWA_GROUP // 2
    group_lanes = SWA_GROUP * WINDOW
    ones_rows = jnp.ones((SUM_ROWS, 2 * WINDOW), BF16)

    def window_start(qb):
        nb = pl.program_id(1) * SWA_BLOCKS_PER_STEP + qb
        return nb, pl.multiple_of(jnp.maximum(nb - 1, 0) * WINDOW, WINDOW)

    def scores(qb, kh):
        _, start = window_start(qb)
        kwin = k_ref[kh, pl.ds(start, 2 * WINDOW), :]
        cols = []
        for cpair in range(pairs_per_kv):
            qt = q_ref[kh * pairs_per_kv + cpair, qb * WINDOW:(qb + 1) * WINDOW, :].T
            zero = jnp.zeros_like(qt)
            cols.append(jnp.where(dim < HEAD_DIM, qt, zero))
            cols.append(jnp.where(dim >= HEAD_DIM, qt, zero))
        return jnp.dot(kwin, jnp.concatenate(cols, axis=1),
                       preferred_element_type=F32)

    def softmax_pv(qb, kh, s_ref):
        nb, start = window_start(qb)
        bias = bias_ref.at[jnp.minimum(nb, 1) if qb == 0 else 1]
        vt = v_ref[pl.ds(start, 2 * WINDOW), :].T[kh * HEAD_DIM:(kh + 1) * HEAD_DIM]
        v_ext = jnp.concatenate([vt, ones_rows], axis=0)
        for cpair in range(pairs_per_kv):
            lanes = slice(2 * cpair * WINDOW, (2 * cpair + 2) * WINDOW)
            sink = jnp.concatenate(
                [jnp.full((1, WINDOW), sinks_ref[kh * SWA_GROUP + 2 * cpair + g] * LOG2E, F32)
                 for g in range(2)], axis=1)
            b0 = kh * group_lanes + 2 * cpair * WINDOW
            s = s_ref[:, lanes] + bias[:, b0:b0 + 2 * WINDOW]
            m = jnp.maximum(jnp.max(s, axis=0, keepdims=True), sink)
            p = jnp.exp2(s - m).astype(BF16)
            o = jnp.dot(v_ext, p, preferred_element_type=F32)
            denom = o[HEAD_DIM:HEAD_DIM + 1] + jnp.exp2(sink - m)
            o = o[:HEAD_DIM] * (1.0 / denom)
            pair = jnp.concatenate([o[:, :WINDOW], o[:, WINDOW:]], axis=0)
            col = (kh * pairs_per_kv + cpair) * LANES
            o_ref[qb * WINDOW:(qb + 1) * WINDOW, col:col + LANES] = pair.T.astype(BF16)

    items = [(qb, kh) for qb in range(SWA_BLOCKS_PER_STEP) for kh in range(SWA_KV_HEADS)]
    s_scr[0] = scores(*items[0])
    for n, item in enumerate(items):
        if n + 1 < len(items):
            s_scr[(n + 1) % 2] = scores(*items[n + 1])
        softmax_pv(*item, s_scr.at[n % 2])


def _swa_bias():
    key = np.arange(2 * WINDOW)[:, None]
    qry = np.arange(WINDOW)[None, :]
    bias = np.full((2, 2 * WINDOW, SWA_Q_HEADS * WINDOW), -np.inf, np.float32)
    for case, offset in enumerate((0, WINDOW)):
        dist = offset + qry - key
        valid = (dist >= 0) & (dist < WINDOW)
        for h, slope in enumerate(_alibi_slopes(SWA_Q_HEADS)):
            bias[case, :, h * WINDOW:(h + 1) * WINDOW] = np.where(valid, -slope * LOG2E * dist, -np.inf)
    return jnp.asarray(bias)


def _attn_b(q3, k3, v3, sinks, B, S):
    T = B * S
    rows = SWA_BLOCKS_PER_STEP * WINDOW
    nstep = S // rows
    n_pairs = SWA_Q_HEADS // 2
    return pl.pallas_call(
        _attn_b_kernel,
        grid_spec=pltpu.PrefetchScalarGridSpec(
            num_scalar_prefetch=1,
            grid=(B, nstep),
            in_specs=[
                pl.BlockSpec((n_pairs, rows, LANES), lambda b, n, *_: (0, b * nstep + n, 0)),
                pl.BlockSpec((SWA_KV_HEADS, S, LANES), lambda b, n, *_: (0, b, 0)),
                pl.BlockSpec((None, S, LANES), lambda b, n, *_: (SWA_KV_HEADS, b, 0)),
                _resident((2, 2 * WINDOW, SWA_Q_HEADS * WINDOW)),
            ],
            out_specs=pl.BlockSpec((rows, D_MODEL), lambda b, n, *_: (b * nstep + n, 0)),
            scratch_shapes=[pltpu.VMEM((2, 2 * WINDOW, SWA_GROUP * WINDOW), F32)],
        ),
        out_shape=jax.ShapeDtypeStruct((T, D_MODEL), BF16),
        compiler_params=_params(2),
        name="swa_attn",
    )(sinks, q3, k3, v3, _swa_bias())


def _gelu_tanh_of_half(h):
    c = math.sqrt(2.0 / math.pi)
    return h + h * jnp.tanh(h * (2.0 * c + (8.0 * 0.044715 * c) * (h * h)))


def _ffn_kernel(o_ref, x_ref, g1_ref, g2_ref, g3_ref, wo_ref, wgu_ref, wd_ref, cw_ref, cb_ref, out_ref,
                x1_scr, h_scr, a_scr, carry_scr, gate0_scr, gate1_scr, up0_scr, up1_scr,
                *, tiles_per_seq):
    i = pl.program_id(0)
    tm = x_ref.shape[0]
    rb = FF_ROW_BLOCK
    n_blocks = tm // rb
    tf = FF_CHUNK
    slots = ((gate0_scr, up0_scr), (gate1_scr, up1_scr))

    @pl.when(i == 0)
    def _():
        carry_scr[...] = jnp.zeros(carry_scr.shape, F32)

    seq_start = (i % tiles_per_seq) == 0

    def mix_norm(r):
        rows = slice(r * rb, (r + 1) * rb)
        mix = jnp.dot(o_ref[rows, :], wo_ref[...], preferred_element_type=F32)
        x1 = x_ref[rows, :] + _rms(mix, g1_ref[...], NORM_EPS)
        x1_scr[rows, :] = x1
        h_scr[rows, :] = _rms(x1, g2_ref[...], NORM_EPS).astype(BF16)

    def up_proj(r):
        gate_scr, up_scr = slots[r % 2]
        if r == 0:
            gate_scr[0:8, :] = jnp.where(seq_start, 0.0, carry_scr[...])
        else:
            gate_scr[0:8, :] = slots[(r - 1) % 2][0][rb:rb + 8, :]
        h = h_scr[r * rb:(r + 1) * rb, :]
        for c in range(0, D_FF, FF_DOT_COLS):
            w = min(FF_DOT_COLS, D_FF - c)
            gate_scr[8:8 + rb, c:c + w] = jnp.dot(h, wgu_ref[:, c:c + w], preferred_element_type=F32)
            up_scr[:, c:c + w] = jnp.dot(h, wgu_ref[:, D_FF + c:D_FF + c + w],
                                         preferred_element_type=F32)

    def act_down(r):
        gate_scr, up_scr = slots[r % 2]
        rows = slice(r * rb, (r + 1) * rb)
        for c in range(0, D_FF, tf):
            cols = slice(c, c + tf)
            cw = 0.5 * cw_ref[:, cols]
            half_conv = (cw[2:3] * gate_scr[8:8 + rb, cols] + 0.5 * cb_ref[:, cols]
                         + cw[1:2] * gate_scr[7:7 + rb, cols]
                         + cw[0:1] * gate_scr[6:6 + rb, cols])
            a_scr[:, cols] = (_gelu_tanh_of_half(half_conv) * up_scr[:, cols]).astype(BF16)
        f = jnp.dot(a_scr[...], wd_ref[...], preferred_element_type=F32)
        out_ref[rows, :] = x1_scr[rows, :] + _rms(f, g3_ref[...], NORM_EPS)
        if r == n_blocks - 1:
            carry_scr[...] = gate_scr[rb:rb + 8, :]

    for r in range(n_blocks):
        mix_norm(r)
    up_proj(0)
    for r in range(n_blocks):
        if r + 1 < n_blocks:
            up_proj(r + 1)
        act_down(r)


def _post_ffn(o, x, g1, g2, g3, w_o, w_up, conv_w, conv_b, w_down, S, in_place):
    T, D = x.shape
    tm = FF_ROW_TILE
    rb = FF_ROW_BLOCK
    kernel = functools.partial(_ffn_kernel, tiles_per_seq=S // tm)
    return pl.pallas_call(
        kernel,
        grid=(T // tm,),
        in_specs=[
            pl.BlockSpec((tm, D), lambda i: (i, 0)),
            pl.BlockSpec((tm, D), lambda i: (i, 0)),
            _resident((1, D)),
            _resident((1, D)),
            _resident((1, D)),
            _resident((D, D)),
            _resident((D, 2 * D_FF)),
            _resident((D_FF, D)),
            _resident((CONV_WIDTH, D_FF)),
            _resident((1, D_FF)),
        ],
        out_specs=pl.BlockSpec((tm, D), lambda i: (i, 0)),
        out_shape=jax.ShapeDtypeStruct((T, D), F32),
        scratch_shapes=[
            pltpu.VMEM((tm, D), F32),
            pltpu.VMEM((tm, D), BF16),
            pltpu.VMEM((rb, D_FF), BF16),
            pltpu.VMEM((8, D_FF), F32),
            pltpu.VMEM((rb + 8, D_FF), F32),
            pltpu.VMEM((rb + 8, D_FF), F32),
            pltpu.VMEM((rb, D_FF), F32),
            pltpu.VMEM((rb, D_FF), F32),
        ],
        input_output_aliases={1: 0} if in_place else {},
        compiler_params=pltpu.CompilerParams(
            dimension_semantics=("arbitrary",), vmem_limit_bytes=VMEM_LIMIT_BYTES,
            allow_input_fusion=[False] * 5 + [True] * 3 + [False] * 2),
        name="post_ffn",
    )(o, x, g1.reshape(1, D), g2.reshape(1, D), g3.reshape(1, D), w_o.astype(BF16),
      w_up.astype(BF16), w_down.astype(BF16), conv_w, conv_b.reshape(1, D_FF))


def _dup_heads(w, n_heads):
    lead = w.shape[:-1]
    w = w.reshape(lead + (n_heads, 1, HEAD_DIM))
    return jnp.broadcast_to(w, lead + (n_heads, 2, HEAD_DIM)).reshape(lead + (2 * n_heads * HEAD_DIM,))


def kernel(x, norm_gains, w_qkv_a, lambda_qk_a, subln_a, w_o_a, kv_norm, w_kv_b, b_kv_b, w_q_b, b_q_b,
           sinks_b, w_o_b, w_up, conv_w, conv_b, w_down):
    B, S, D = x.shape
    T = B * S
    assert D == D_MODEL and S % ATTN_QUERY_TILE == 0 and S % ROW_TILE == 0 and S % FF_ROW_TILE == 0 and S >= 2 * WINDOW
    xt = x.reshape(T, D)
    kv3 = None
    for layer in range(DEPTH):
        g = norm_gains[layer]
        if layer < N_A_LAYERS:
            lambda_init = 0.8 - 0.6 * math.exp(-0.3 * layer)
            w_q, w_k, w_v = jnp.split(w_qkv_a[layer].astype(BF16), 3, axis=1)
            qv_t, k3 = _proj(xt, [
                (g[0], jnp.concatenate([w_q, w_v], axis=1), jnp.zeros((2 * D,), F32),
                 DIFF_HEADS, QK_SCALE * LOG2E, True),
                (g[0], w_k, jnp.zeros((D,), F32), 0, 1.0, False)])
            o = _attn_a(qv_t, k3, lambda_qk_a[layer], subln_a[layer], lambda_init, B, S)
            w_o = w_o_a[layer]
        else:
            j = layer - N_A_LAYERS
            q_group = (g[0], w_q_b[j].astype(BF16), b_q_b[j], SWA_Q_HEADS // 2, QK_SCALE * LOG2E, False)
            if kv3 is None:
                kvw = SWA_KV_HEADS * HEAD_DIM
                w_kv = jnp.concatenate([_dup_heads(w_kv_b[:, :kvw], SWA_KV_HEADS), w_kv_b[:, kvw:]], axis=1)
                b_kv = jnp.concatenate([_dup_heads(b_kv_b[:kvw], SWA_KV_HEADS), b_kv_b[kvw:]])
                q3, kv3 = _proj(xt, [q_group, (kv_norm, w_kv.astype(BF16), b_kv, 0, 1.0, False)])
            else:
                q3, = _proj(xt, [q_group])
            o = _attn_b(q3, kv3, kv3, sinks_b[j], B, S)
            w_o = w_o_b[j]
        xt = _post_ffn(o, xt, g[1], g[2], g[3], w_o, w_up[layer], conv_w[layer], conv_b[layer],
                       w_down[layer], S, in_place=layer > 0)
    return xt.reshape(B, S, D)
```
